```python
import jax, jax.numpy as jnp
from jax import lax
import numpy as np

D_MODEL = 1024
BATCH = 16
SEQ = 2048
DEPTH = 4
DEC_BATCH = 32
DEC_SEQ = 16
PAST_LEN = 2048

CHUNK = 64
D_A = D_MODEL
D_B = D_MODEL
CONV_A_W = 3
CONV_B_W = 31
POOL_WINDOWS = (2, 4, 8, 16)
N_POOL_GROUPS = len(POOL_WINDOWS)
POOL_G = D_MODEL // N_POOL_GROUPS
POOL_PREV = max(POOL_WINDOWS) - 1
D_FF = 4 * D_MODEL
N_CONV_LAYERS = (DEPTH + 1) // 2
N_POOL_LAYERS = DEPTH // 2
D_IN_CONV = 3 * D_A + 2 * D_B
RMS_EPS = 1e-6
LN_EPS = 1e-5

kernel_name = "hybrid_streaming_conv_pool_encoder_step"


def _rmsnorm(x, g):
    x32 = x.astype(jnp.float32)
    y = x32 * lax.rsqrt(jnp.mean(x32 * x32, axis=-1, keepdims=True) + RMS_EPS) * g.astype(jnp.float32)
    return y.astype(x.dtype)


def _layernorm(x, g, b):
    x32 = x.astype(jnp.float32)
    mu = jnp.mean(x32, axis=-1, keepdims=True)
    var = jnp.mean(jnp.square(x32 - mu), axis=-1, keepdims=True)
    y = (x32 - mu) * lax.rsqrt(var + LN_EPS) * g.astype(jnp.float32) + b.astype(jnp.float32)
    return y.astype(x.dtype)


def _swiglu(u, w_gate, w_up, w_down):
    h = jax.nn.silu(u @ w_gate) * (u @ w_up)
    return h @ w_down


def _causal_dwconv(u, prev, w):
    u_ext = jnp.concatenate([prev.astype(u.dtype), u], axis=1)
    y = lax.conv_general_dilated(u_ext, w[:, None, :].astype(u.dtype), window_strides=(1,), padding='VALID',
                                 dimension_numbers=('NWC', 'WIO', 'NWC'), feature_group_count=u.shape[-1])
    return y, u_ext[:, -(w.shape[0] - 1):]


def _conv_mixer(u, prev_a, prev_b, w_in, conv_a_w, conv_b_w, ln_g, ln_b, w_out):
    z = u @ w_in
    h_a, b_a, c_a, v_b, g_b = jnp.split(z, [D_A, 2 * D_A, 3 * D_A, 3 * D_A + D_B], axis=-1)
    ya, new_a = _causal_dwconv(c_a * h_a, prev_a, conv_a_w)
    ya = b_a * ya
    yb, new_b = _causal_dwconv(v_b * jax.nn.sigmoid(g_b), prev_b, conv_b_w)
    yb = jax.nn.silu(_layernorm(yb, ln_g, ln_b))
    y = jnp.concatenate([ya, yb], axis=-1) @ w_out
    return y, new_a, new_b


def _pool_mixer(u, prev, start_pos, w_pool, scale):
    bsz, t_len, _ = u.shape
    u_ext = jnp.concatenate([prev.astype(u.dtype), u], axis=1)
    c = jnp.cumsum(u_ext.astype(jnp.float32), axis=1)
    c = jnp.pad(c, ((0, 0), (1, 0), (0, 0)))
    pos = (start_pos + jnp.arange(t_len)).astype(jnp.float32)
    u32 = u.astype(jnp.float32)
    outs = []
    for gi, win in enumerate(POOL_WINDOWS):
        sl = slice(gi * POOL_G, (gi + 1) * POOL_G)
        hi = c[:, POOL_PREV + 1:POOL_PREV + 1 + t_len, sl]
        lo = c[:, POOL_PREV + 1 - win:POOL_PREV + 1 - win + t_len, sl]
        cnt = jnp.minimum(jnp.float32(win), pos + 1.0)[None, :, None]
        outs.append((hi - lo) / cnt - u32[:, :, sl])
    p = jnp.stack(outs, axis=2)
    y = jnp.einsum('btgi,gio->btgo', p, w_pool.astype(jnp.float32)).reshape(bsz, t_len, D_MODEL)
    y = y * scale.astype(jnp.float32)
    return y.astype(u.dtype), u_ext[:, -POOL_PREV:]


def _trunk(x, st_a, st_b, st_p, start_pos, norm_g, final_norm_g, w_ffn_gate, w_ffn_up, w_ffn_down,
           w_in_conv, conv_a_w, conv_b_w, ln_b_g, ln_b_b, w_out_conv, w_pool, pool_scale):
    new_a, new_b, new_p = [], [], []
    for layer in range(DEPTH):
        x = x + 0.5 * _swiglu(_rmsnorm(x, norm_g[layer, 0]), w_ffn_gate[layer, 0], w_ffn_up[layer, 0], w_ffn_down[layer, 0])
        u = _rmsnorm(x, norm_g[layer, 1])
        i = layer // 2
        if layer % 2 == 0:
            y, sa, sb = _conv_mixer(u, st_a[i], st_b[i], w_in_conv[i], conv_a_w[i], conv_b_w[i],
                                    ln_b_g[i], ln_b_b[i], w_out_conv[i])
            new_a.append(sa)
            new_b.append(sb)
        else:
            y, sp = _pool_mixer(u, st_p[i], start_pos, w_pool[i], pool_scale[i])
            new_p.append(sp)
        x = x + y
        x = x + 0.5 * _swiglu(_rmsnorm(x, norm_g[layer, 2]), w_ffn_gate[layer, 1], w_ffn_up[layer, 1], w_ffn_down[layer, 1])
    return _rmsnorm(x, final_norm_g), jnp.stack(new_a), jnp.stack(new_b), jnp.stack(new_p)


def setup_inputs(seed: int = 0) -> dict:
    key = jax.random.key(seed)
    ks = jax.random.split(key, 20)
    nrm = lambda k, s, sc: jax.random.normal(k, s, jnp.float32) * sc
    return {
        "x_prompt": nrm(ks[0], (BATCH, SEQ, D_MODEL), 1.0),
        "x_sample": nrm(ks[1], (DEC_BATCH, DEC_SEQ, D_MODEL), 1.0),
        "state_conv_a": nrm(ks[2], (N_CONV_LAYERS, DEC_BATCH, CONV_A_W - 1, D_A), 1.0),
        "state_conv_b": nrm(ks[3], (N_CONV_LAYERS, DEC_BATCH, CONV_B_W - 1, D_B), 1.0),
        "state_pool": nrm(ks[4], (N_POOL_LAYERS, DEC_BATCH, POOL_PREV, D_MODEL), 1.0),
        "norm_g": 1.0 + nrm(ks[5], (DEPTH, 3, D_MODEL), 0.05),
        "final_norm_g": 1.0 + nrm(ks[6], (D_MODEL,), 0.05),
        "w_ffn_gate": nrm(ks[7], (DEPTH, 2, D_MODEL, D_FF), D_MODEL ** -0.5),
        "w_ffn_up": nrm(ks[8], (DEPTH, 2, D_MODEL, D_FF), D_MODEL ** -0.5),
        "w_ffn_down": nrm(ks[9], (DEPTH, 2, D_FF, D_MODEL), D_FF ** -0.5),
        "w_in_conv": nrm(ks[10], (N_CONV_LAYERS, D_MODEL, D_IN_CONV), D_MODEL ** -0.5),
        "conv_a_w": nrm(ks[11], (N_CONV_LAYERS, CONV_A_W, D_A), CONV_A_W ** -0.5),
        "conv_b_w": nrm(ks[12], (N_CONV_LAYERS, CONV_B_W, D_B), CONV_B_W ** -0.5),
        "ln_b_g": 1.0 + nrm(ks[13], (N_CONV_LAYERS, D_B), 0.05),
        "ln_b_b": nrm(ks[14], (N_CONV_LAYERS, D_B), 0.02),
        "w_out_conv": nrm(ks[15], (N_CONV_LAYERS, D_A + D_B, D_MODEL), (D_A + D_B) ** -0.5),
        "w_pool": nrm(ks[16], (N_POOL_LAYERS, N_POOL_GROUPS, POOL_G, POOL_G), POOL_G ** -0.5),
        "pool_scale": 0.5 + nrm(ks[17], (N_POOL_LAYERS, D_MODEL), 0.1),
    }


def reference(x_prompt, x_sample, state_conv_a, state_conv_b, state_pool, norm_g, final_norm_g,
              w_ffn_gate, w_ffn_up, w_ffn_down, w_in_conv, conv_a_w, conv_b_w, ln_b_g, ln_b_b,
              w_out_conv, w_pool, pool_scale):
    assert x_sample.shape[1] <= CHUNK
    weights = (norm_g, final_norm_g, w_ffn_gate, w_ffn_up, w_ffn_down, w_in_conv, conv_a_w, conv_b_w,
               ln_b_g, ln_b_b, w_out_conv, w_pool, pool_scale)
    b = x_prompt.shape[0]
    dt = x_prompt.dtype
    z_a = jnp.zeros((N_CONV_LAYERS, b, CONV_A_W - 1, D_A), dt)
    z_b = jnp.zeros((N_CONV_LAYERS, b, CONV_B_W - 1, D_B), dt)
    z_p = jnp.zeros((N_POOL_LAYERS, b, POOL_PREV, D_MODEL), dt)
    y_prompt, pa_a, pa_b, pa_p = _trunk(x_prompt, z_a, z_b, z_p, 0, *weights)
    y_sample, sa_a, sa_b, sa_p = _trunk(x_sample, state_conv_a, state_conv_b, state_pool, PAST_LEN, *weights)
    return (y_prompt, y_sample, pa_a, sa_a, pa_b, sa_b, pa_p, sa_p)
```

```python
import functools

import jax
import jax.numpy as jnp
from jax import lax
from jax.experimental import pallas as pl
from jax.experimental.pallas import tpu as pltpu

D_MODEL = 1024
D_FF = 4 * D_MODEL
DEPTH = 4
CONV_A_W = 3
CONV_B_W = 31
POOL_WINDOWS = (2, 4, 8, 16)
POOL_G = D_MODEL // len(POOL_WINDOWS)
POOL_PREV = max(POOL_WINDOWS) - 1
PAST_LEN = 2048
RMS_EPS = 1e-6
LN_EPS = 1e-5

SUBLANES = 8
FFN_TM = 512
FFN_CHUNK = 512
CONV_TT = 512
POOL_TT = 512
A_PAD = SUBLANES
B_PAD = 4 * SUBLANES
P_PAD = 2 * SUBLANES
VMEM_LIMIT = 56 * 1024 * 1024

F32 = jnp.float32
BF16 = jnp.bfloat16


def _rms(x, g):
    ms = jnp.mean(x * x, axis=-1, keepdims=True)
    return x * lax.rsqrt(ms + RMS_EPS) * g


def _resident(shape):
    zeros = (0,) * len(shape)
    return pl.BlockSpec(shape, lambda *_: zeros, pipeline_mode=pl.Buffered(1))


def _ffn_body(x_ref, g_ref, wg_ref, wu_ref, wd_ref, fg_ref, o_ref, u_scr, acc_scr, *, n_chunks, final):
    u_scr[...] = _rms(x_ref[...], g_ref[...]).astype(BF16)
    acc_scr[...] = jnp.zeros_like(acc_scr)

    def step(j, carry):
        u = u_scr[...]
        gate = jnp.dot(u, wg_ref[j], preferred_element_type=F32)
        up = jnp.dot(u, wu_ref[j], preferred_element_type=F32)
        h = (gate * jax.nn.sigmoid(gate) * up).astype(BF16)
        acc_scr[...] += jnp.dot(h, wd_ref[j], preferred_element_type=F32)
        return carry

    lax.fori_loop(0, n_chunks, step, 0)
    y = x_ref[...] + 0.5 * acc_scr[...]
    if final:
        y = _rms(y, fg_ref[...])
    o_ref[...] = y


def _ffn(x, g, wg, wu, wd, fg, *, final):
    n = x.shape[0]
    tm = min(FFN_TM, n)
    n_chunks, _, c = wg.shape
    row = pl.BlockSpec((tm, D_MODEL), lambda i: (i, 0))
    return pl.pallas_call(
        functools.partial(_ffn_body, n_chunks=n_chunks, final=final),
        grid=(n // tm,),
        in_specs=[row, _resident((1, D_MODEL)), _resident(wg.shape), _resident(wu.shape),
                  _resident(wd.shape), _resident((1, D_MODEL))],
        out_specs=row,
        out_shape=jax.ShapeDtypeStruct((n, D_MODEL), F32),
        scratch_shapes=[pltpu.VMEM((tm, D_MODEL), BF16), pltpu.VMEM((tm, D_MODEL), F32)],
        compiler_params=pltpu.CompilerParams(dimension_semantics=("arbitrary",),
                                             vmem_limit_bytes=VMEM_LIMIT),
        name="ffn_final" if final else "ffn",
    )(x, g, wg, wu, wd, fg)


def _conv_body(x_ref, pa_ref, pb_ref, g_ref, win_ref, caw_ref, cbw_ref, lng_ref, lnb_ref, wout_ref,
               o_ref, na_ref, nb_ref, ext_a, ext_b, *, nb, tt):
    m = nb * tt

    @pl.when(pl.program_id(1) == 0)
    def _():
        ext_a[:, A_PAD - (CONV_A_W - 1):A_PAD, :] = pa_ref[...]
        ext_b[:, B_PAD - (CONV_B_W - 1):B_PAD, :] = pb_ref[...]

    x = x_ref[...].reshape(m, D_MODEL)
    u = _rms(x, g_ref[...]).astype(BF16)

    def proj(i):
        return jnp.dot(u, win_ref[i], preferred_element_type=F32)

    h_a = proj(0)
    b_a = proj(1)
    c_a = proj(2)
    ext_a[:, A_PAD:A_PAD + tt, :] = (c_a * h_a).reshape(nb, tt, D_MODEL)
    ya = None
    for k in range(CONV_A_W):
        lo = A_PAD - (CONV_A_W - 1) + k
        term = caw_ref[k:k + 1, :] * ext_a[:, lo:lo + tt, :]
        ya = term if ya is None else ya + term
    ya = b_a * ya.reshape(m, D_MODEL)

    v_b = proj(3)
    g_b = proj(4)
    ext_b[:, B_PAD:B_PAD + tt, :] = (v_b * jax.nn.sigmoid(g_b)).reshape(nb, tt, D_MODEL)
    yb = None
    for k in range(CONV_B_W):
        lo = B_PAD - (CONV_B_W - 1) + k
        term = cbw_ref[k:k + 1, :] * ext_b[:, lo:lo + tt, :]
        yb = term if yb is None else yb + term
    yb = yb.reshape(m, D_MODEL)
    mu = jnp.mean(yb, axis=-1, keepdims=True)
    var = jnp.mean(jnp.square(yb - mu), axis=-1, keepdims=True)
    yb = (yb - mu) * lax.rsqrt(var + LN_EPS) * lng_ref[...] + lnb_ref[...]
    yb = yb * jax.nn.sigmoid(yb)

    y = jnp.dot(ya.astype(BF16), wout_ref[0], preferred_element_type=F32)
    y = y + jnp.dot(yb.astype(BF16), wout_ref[1], preferred_element_type=F32)
    o_ref[...] = (x + y).reshape(nb, tt, D_MODEL)

    new_a = ext_a[:, A_PAD + tt - (CONV_A_W - 1):A_PAD + tt, :]
    new_b = ext_b[:, B_PAD + tt - (CONV_B_W - 1):B_PAD + tt, :]
    na_ref[...] = new_a
    nb_ref[...] = new_b
    ext_a[:, A_PAD - (CONV_A_W - 1):A_PAD, :] = new_a
    ext_b[:, B_PAD - (CONV_B_W - 1):B_PAD, :] = new_b


def _conv_mixer(x, prev_a, prev_b, g, w_in, caw, cbw, lng, lnb, w_out, *, nb, tt):
    b, t, _ = x.shape
    body = functools.partial(_conv_body, nb=nb, tt=tt)
    xspec = pl.BlockSpec((nb, tt, D_MODEL), lambda i, j: (i, j, 0))

    def state(rows):
        return pl.BlockSpec((nb, rows, D_MODEL), lambda i, j: (i, 0, 0))

    return pl.pallas_call(
        body,
        grid=(b // nb, t // tt),
        in_specs=[xspec, state(CONV_A_W - 1), state(CONV_B_W - 1), _resident((1, D_MODEL)),
                  _resident(w_in.shape), _resident(caw.shape), _resident(cbw.shape),
                  _resident((1, D_MODEL)), _resident((1, D_MODEL)), _resident(w_out.shape)],
        out_specs=[xspec, state(CONV_A_W - 1), state(CONV_B_W - 1)],
        out_shape=[jax.ShapeDtypeStruct(x.shape, F32),
                   jax.ShapeDtypeStruct((b, CONV_A_W - 1, D_MODEL), F32),
                   jax.ShapeDtypeStruct((b, CONV_B_W - 1, D_MODEL), F32)],
        scratch_shapes=[pltpu.VMEM((nb, A_PAD + tt, D_MODEL), F32),
                        pltpu.VMEM((nb, B_PAD + tt, D_MODEL), F32)],
        compiler_params=pltpu.CompilerParams(dimension_semantics=("arbitrary", "arbitrary"),
                                             vmem_limit_bytes=VMEM_LIMIT),
        name="conv_mixer",
    )(x, prev_a, prev_b, g, w_in, caw, cbw, lng, lnb, w_out)


def _pool_body(x_ref, pp_ref, g_ref, wp_ref, sc_ref, o_ref, np_ref, ext, *, nb, tt, start_pos):
    m = nb * tt
    t_idx = pl.program_id(1)

    @pl.when(t_idx == 0)
    def _():
        ext[:, P_PAD - POOL_PREV:P_PAD, :] = pp_ref[...]

    x = x_ref[...].reshape(m, D_MODEL)
    u = _rms(x, g_ref[...])
    ext[:, P_PAD:P_PAD + tt, :] = u.reshape(nb, tt, D_MODEL)

    pos = start_pos + t_idx * tt + lax.broadcasted_iota(jnp.int32, (1, tt, 1), 1)
    pos = pos.astype(F32)
    ys = []
    for gi, win in enumerate(POOL_WINDOWS):
        c0 = gi * POOL_G
        s = None
        for k in range(win - 1, -1, -1):
            term = ext[:, P_PAD - k:P_PAD - k + tt, c0:c0 + POOL_G]
            s = term if s is None else s + term
        cnt = jnp.minimum(jnp.float32(win), pos + 1.0)
        p = s / cnt - ext[:, P_PAD:P_PAD + tt, c0:c0 + POOL_G]
        ys.append(jnp.dot(p.reshape(m, POOL_G).astype(BF16), wp_ref[gi], preferred_element_type=F32))
    y = jnp.concatenate(ys, axis=-1) * sc_ref[...]
    o_ref[...] = (x + y).reshape(nb, tt, D_MODEL)

    new_p = ext[:, P_PAD + tt - POOL_PREV:P_PAD + tt, :]
    np_ref[...] = new_p
    ext[:, P_PAD - POOL_PREV:P_PAD, :] = new_p


def _pool_mixer(x, prev_p, g, wp, sc, *, nb, tt, start_pos):
    b, t, _ = x.shape
    body = functools.partial(_pool_body, nb=nb, tt=tt, start_pos=start_pos)
    xspec = pl.BlockSpec((nb, tt, D_MODEL), lambda i, j: (i, j, 0))
    sspec = pl.BlockSpec((nb, POOL_PREV, D_MODEL), lambda i, j: (i, 0, 0))
    return pl.pallas_call(
        body,
        grid=(b // nb, t // tt),
        in_specs=[xspec, sspec, _resident((1, D_MODEL)), _resident(wp.shape), _resident((1, D_MODEL))],
        out_specs=[xspec, sspec],
        out_shape=[jax.ShapeDtypeStruct(x.shape, F32),
                   jax.ShapeDtypeStruct((b, POOL_PREV, D_MODEL), F32)],
        scratch_shapes=[pltpu.VMEM((nb, P_PAD + tt, D_MODEL), F32)],
        compiler_params=pltpu.CompilerParams(dimension_semantics=("arbitrary", "arbitrary"),
                                             vmem_limit_bytes=VMEM_LIMIT),
        name="pool_mixer",
    )(x, prev_p, g, wp, sc)


def _trunk(x, st_a, st_b, st_p, start_pos, w, *, nb, tt_conv, tt_pool):
    b, t, _ = x.shape
    new_a, new_b, new_p = [], [], []
    for layer in range(DEPTH):
        i = layer // 2
        x = _ffn(x.reshape(b * t, D_MODEL), w["norm_g"][layer, 0], w["wg"][layer, 0], w["wu"][layer, 0],
                 w["wd"][layer, 0], w["final_g"], final=False).reshape(b, t, D_MODEL)
        if layer % 2 == 0:
            x, sa, sb = _conv_mixer(x, st_a[i], st_b[i], w["norm_g"][layer, 1], w["w_in"][i], w["caw"][i],
                                    w["cbw"][i], w["lng"][i], w["lnb"][i], w["w_out"][i], nb=nb, tt=tt_conv)
            new_a.append(sa)
            new_b.append(sb)
        else:
            x, sp = _pool_mixer(x, st_p[i], w["norm_g"][layer, 1], w["wp"][i], w["psc"][i],
                                nb=nb, tt=tt_pool, start_pos=start_pos)
            new_p.append(sp)
        x = _ffn(x.reshape(b * t, D_MODEL), w["norm_g"][layer, 2], w["wg"][layer, 1], w["wu"][layer, 1],
                 w["wd"][layer, 1], w["final_g"], final=(layer == DEPTH - 1)).reshape(b, t, D_MODEL)
    return x, jnp.stack(new_a), jnp.stack(new_b), jnp.stack(new_p)


def kernel(x_prompt, x_sample, state_conv_a, state_conv_b, state_pool, norm_g, final_norm_g, w_ffn_gate,
           w_ffn_up, w_ffn_down, w_in_conv, conv_a_w, conv_b_w, ln_b_g, ln_b_b, w_out_conv, w_pool,
           pool_scale):
    n_chunks = D_FF // FFN_CHUNK
    n_conv = w_in_conv.shape[0]

    def col_chunks(wt):
        wt = wt.astype(BF16).reshape(DEPTH, 2, D_MODEL, n_chunks, FFN_CHUNK)
        return wt.transpose(0, 1, 3, 2, 4)

    w = {
        "norm_g": norm_g.reshape(DEPTH, 3, 1, D_MODEL),
        "final_g": final_norm_g.reshape(1, D_MODEL),
        "wg": col_chunks(w_ffn_gate),
        "wu": col_chunks(w_ffn_up),
        "wd": w_ffn_down.astype(BF16).reshape(DEPTH, 2, n_chunks, FFN_CHUNK, D_MODEL),
        "w_in": w_in_conv.astype(BF16).reshape(n_conv, D_MODEL, 5, D_MODEL).transpose(0, 2, 1, 3),
        "caw": conv_a_w,
        "cbw": conv_b_w,
        "lng": ln_b_g.reshape(n_conv, 1, D_MODEL),
        "lnb": ln_b_b.reshape(n_conv, 1, D_MODEL),
        "w_out": w_out_conv.astype(BF16).reshape(n_conv, 2, D_MODEL, D_MODEL),
        "wp": w_pool.astype(BF16),
        "psc": pool_scale.reshape(-1, 1, D_MODEL),
    }
    b = x_prompt.shape[0]
    z_a = jnp.zeros((n_conv, b, CONV_A_W - 1, D_MODEL), F32)
    z_b = jnp.zeros((n_conv, b, CONV_B_W - 1, D_MODEL), F32)
    z_p = jnp.zeros((w_pool.shape[0], b, POOL_PREV, D_MODEL), F32)
    y_p, pa_a, pa_b, pa_p = _trunk(x_prompt, z_a, z_b, z_p, 0, w, nb=1, tt_conv=CONV_TT, tt_pool=POOL_TT)
    db, dt, _ = x_sample.shape
    y_s, sa_a, sa_b, sa_p = _trunk(x_sample, state_conv_a, state_conv_b, state_pool, PAST_LEN, w,
                                   nb=db, tt_conv=dt, tt_pool=dt)
    return (y_p, y_s, pa_a, sa_a, pa_b, sa_b, pa_p, sa_p)
```

```python
import functools

import jax
import jax.numpy as jnp
from jax import lax
from jax.experimental import pallas as pl
from jax.experimental.pallas import tpu as pltpu

D_MODEL = 1024
D_FF = 4 * D_MODEL
DEPTH = 4
CONV_A_W = 3
CONV_B_W = 31
POOL_WINDOWS = (2, 4, 8, 16)
POOL_G = D_MODEL // len(POOL_WINDOWS)
POOL_PREV = max(POOL_WINDOWS) - 1
PAST_LEN = 2048
RMS_EPS = 1e-6
LN_EPS = 1e-5

SUBLANES = 8
LANES = 128
MXU_N = 256
FFN_TM = 512
FFN_CHUNK = 512
MIX_ROWS = 512
CONV_OUT_BLOCK = 8
VMEM_LIMIT = 56 * 1024 * 1024

F32 = jnp.float32
BF16 = jnp.bfloat16


def _rms(x, g):
    ms = jnp.mean(x * x, axis=-1, keepdims=True)
    return x * lax.rsqrt(ms + RMS_EPS) * g


def _resident(shape):
    zeros = (0,) * len(shape)
    return pl.BlockSpec(shape, lambda *_: zeros, pipeline_mode=pl.Buffered(1))


def _ffn_body(x_ref, g_ref, wg_ref, wu_ref, wd_ref, fg_ref, o_ref, u_scr, *, n_chunks, final):
    u_scr[...] = _rms(x_ref[...], g_ref[...]).astype(BF16)
    for j in range(n_chunks):
        u = u_scr[...]
        gate = jnp.dot(u, wg_ref[j], preferred_element_type=F32)
        up = jnp.dot(u, wu_ref[j], preferred_element_type=F32)
        h = (gate * jax.nn.sigmoid(gate) * (0.5 * up)).astype(BF16)
        d = jnp.dot(h, wd_ref[j], preferred_element_type=F32)
        if j == 0:
            o_ref[...] = x_ref[...] + d
        else:
            o_ref[...] += d
    if final:
        o_ref[...] = _rms(o_ref[...], fg_ref[...])


def _ffn(x, g, wg, wu, wd, fg, *, final):
    n = x.shape[0]
    tm = min(FFN_TM, n)
    n_chunks = wg.shape[0]
    row = pl.BlockSpec((tm, D_MODEL), lambda i: (i, 0))
    return pl.pallas_call(
        functools.partial(_ffn_body, n_chunks=n_chunks, final=final),
        grid=(n // tm,),
        in_specs=[row, _resident((1, D_MODEL)), _resident(wg.shape), _resident(wu.shape),
                  _resident(wd.shape), _resident((1, D_MODEL))],
        out_specs=row,
        out_shape=jax.ShapeDtypeStruct((n, D_MODEL), F32),
        scratch_shapes=[pltpu.VMEM((tm, D_MODEL), BF16)],
        compiler_params=pltpu.CompilerParams(dimension_semantics=("arbitrary",),
                                             vmem_limit_bytes=VMEM_LIMIT),
        name="ffn_final" if final else "ffn",
    )(x, g, wg, wu, wd, fg)


def _dwconv_block(w_ref, ext_ref, n_taps, nb, m, c0, width, emit):
    stride = nb // SUBLANES
    n_out = m // nb
    blk = min(CONV_OUT_BLOCK, n_out)
    for lane0 in range(c0, c0 + width, LANES):
        lanes = slice(lane0, lane0 + LANES)
        taps = [jnp.broadcast_to(w_ref[k:k + 1, lanes], (SUBLANES, LANES)) for k in range(n_taps)]
        for phase in range(stride):
            for i0 in range(0, n_out, blk):
                accs = [None] * blk
                for n in range(i0, i0 + blk + n_taps - 1):
                    r = (phase + stride * n) * SUBLANES
                    e = ext_ref[r:r + SUBLANES, lanes]
                    for ii in range(blk):
                        k = n - (i0 + ii)
                        if 0 <= k < n_taps:
                            term = taps[k] * e
                            accs[ii] = term if accs[ii] is None else accs[ii] + term
                for ii in range(blk):
                    emit((phase + stride * (i0 + ii)) * SUBLANES, lanes, accs[ii])


def _conv_body(x_ref, pa_ref, pb_ref, g_ref, win_ref, caw_ref, cbw_ref, lng_ref, lnb_ref, wout_ref,
               o_ref, na_ref, nb_ref, ext_a, ext_b, ya_scr, yb_scr, *, nb, m, carry):
    ha = (CONV_A_W - 1) * nb
    hb = (CONV_B_W - 1) * nb

    @pl.when(pl.program_id(0) == 0)
    def _():
        ext_a[0:ha, :] = pa_ref[...]
        ext_b[0:hb, :] = pb_ref[...]

    u = _rms(x_ref[...], g_ref[...]).astype(BF16)

    def proj(i, cols):
        return jnp.dot(u, win_ref[i, :, cols], preferred_element_type=F32)

    for c0 in range(0, D_MODEL, MXU_N):
        cols = slice(c0, c0 + MXU_N)
        ext_b[hb:hb + m, cols] = proj(3, cols) * jax.nn.sigmoid(proj(4, cols))

        def emit_b(r0, lanes, acc):
            yb_scr[r0:r0 + SUBLANES, lanes] = acc

        _dwconv_block(cbw_ref, ext_b, CONV_B_W, nb, m, c0, MXU_N, emit_b)

        ext_a[ha:ha + m, cols] = proj(2, cols) * proj(0, cols)
        ya_scr[:, cols] = proj(1, cols)

        def emit_a(r0, lanes, acc):
            ya_scr[r0:r0 + SUBLANES, lanes] = ya_scr[r0:r0 + SUBLANES, lanes] * acc

        _dwconv_block(caw_ref, ext_a, CONV_A_W, nb, m, c0, MXU_N, emit_a)

    yb = yb_scr[...]
    mu = jnp.mean(yb, axis=-1, keepdims=True)
    yc = yb - mu
    var = jnp.mean(yc * yc, axis=-1, keepdims=True)
    yb = yc * lax.rsqrt(var + LN_EPS) * lng_ref[...] + lnb_ref[...]
    yb = yb * jax.nn.sigmoid(yb)

    y = jnp.dot(ya_scr[...].astype(BF16), wout_ref[0], preferred_element_type=F32)
    y = y + jnp.dot(yb.astype(BF16), wout_ref[1], preferred_element_type=F32)
    o_ref[...] = x_ref[...] + y

    new_a = ext_a[m:m + ha, :]
    new_b = ext_b[m:m + hb, :]
    na_ref[...] = new_a
    nb_ref[...] = new_b
    if carry:
        ext_a[0:ha, :] = new_a
        ext_b[0:hb, :] = new_b


def _conv_mixer(x, prev_a, prev_b, g, w_in, caw, cbw, lng, lnb, w_out, *, nb, m):
    n = x.shape[0]
    ha = (CONV_A_W - 1) * nb
    hb = (CONV_B_W - 1) * nb
    n_tiles = n // m
    assert n_tiles == 1 or m >= hb
    body = functools.partial(_conv_body, nb=nb, m=m, carry=n_tiles > 1)
    row = pl.BlockSpec((m, D_MODEL), lambda i: (i, 0))
    return pl.pallas_call(
        body,
        grid=(n_tiles,),
        in_specs=[row, _resident((ha, D_MODEL)), _resident((hb, D_MODEL)), _resident((1, D_MODEL)),
                  _resident(w_in.shape), _resident(caw.shape), _resident(cbw.shape),
                  _resident((1, D_MODEL)), _resident((1, D_MODEL)), _resident(w_out.shape)],
        out_specs=[row, pl.BlockSpec((ha, D_MODEL), lambda i: (0, 0)),
                   pl.BlockSpec((hb, D_MODEL), lambda i: (0, 0))],
        out_shape=[jax.ShapeDtypeStruct(x.shape, F32),
                   jax.ShapeDtypeStruct((ha, D_MODEL), F32),
                   jax.ShapeDtypeStruct((hb, D_MODEL), F32)],
        scratch_shapes=[pltpu.VMEM((ha + m, D_MODEL), F32), pltpu.VMEM((hb + m, D_MODEL), F32),
                        pltpu.VMEM((m, D_MODEL), F32), pltpu.VMEM((m, D_MODEL), F32)],
        compiler_params=pltpu.CompilerParams(dimension_semantics=("arbitrary",),
                                             vmem_limit_bytes=VMEM_LIMIT),
        name="conv_mixer",
    )(x, prev_a, prev_b, g, w_in, caw, cbw, lng, lnb, w_out)


def _pool_body(x_ref, pp_ref, g_ref, wp_ref, sc_ref, o_ref, np_ref, ext, *, nb, m, start_pos, carry):
    hp = POOL_PREV * nb
    tt = m // nb
    t_idx = pl.program_id(0)

    @pl.when(t_idx == 0)
    def _():
        ext[0:hp, :] = pp_ref[...]

    x = x_ref[...]
    ext[hp:hp + m, :] = _rms(x, g_ref[...])

    log2_nb = nb.bit_length() - 1
    assert nb == 1 << log2_nb
    step = lax.shift_right_logical(lax.broadcasted_iota(jnp.int32, (m, 1), 0), log2_nb)
    pos = (start_pos + t_idx * tt + step).astype(F32)
    ys = []
    for gi, win in enumerate(POOL_WINDOWS):
        cols = slice(gi * POOL_G, (gi + 1) * POOL_G)
        s = None
        for k in range(win - 1, -1, -1):
            term = ext[hp - k * nb:hp - k * nb + m, cols]
            s = term if s is None else s + term
        cnt = jnp.minimum(jnp.float32(win), pos + 1.0)
        p = s / cnt - ext[hp:hp + m, cols]
        ys.append(jnp.dot(p.astype(BF16), wp_ref[gi], preferred_element_type=F32))
    y = jnp.concatenate(ys, axis=-1) * sc_ref[...]
    o_ref[...] = x + y

    new_p = ext[m:m + hp, :]
    np_ref[...] = new_p
    if carry:
        ext[0:hp, :] = new_p


def _pool_mixer(x, prev_p, g, wp, sc, *, nb, m, start_pos):
    n = x.shape[0]
    hp = POOL_PREV * nb
    n_tiles = n // m
    assert n_tiles == 1 or m >= hp
    body = functools.partial(_pool_body, nb=nb, m=m, start_pos=start_pos, carry=n_tiles > 1)
    row = pl.BlockSpec((m, D_MODEL), lambda i: (i, 0))
    return pl.pallas_call(
        body,
        grid=(n_tiles,),
        in_specs=[row, _resident((hp, D_MODEL)), _resident((1, D_MODEL)), _resident(wp.shape),
                  _resident((1, D_MODEL))],
        out_specs=[row, pl.BlockSpec((hp, D_MODEL), lambda i: (0, 0))],
        out_shape=[jax.ShapeDtypeStruct(x.shape, F32), jax.ShapeDtypeStruct((hp, D_MODEL), F32)],
        scratch_shapes=[pltpu.VMEM((hp + m, D_MODEL), F32)],
        compiler_params=pltpu.CompilerParams(dimension_semantics=("arbitrary",),
                                             vmem_limit_bytes=VMEM_LIMIT),
        name="pool_mixer",
    )(x, prev_p, g, wp, sc)


def _time_major(a):
    a = jnp.swapaxes(a, -3, -2)
    return a.reshape(a.shape[:-3] + (a.shape[-3] * a.shape[-2], a.shape[-1]))


def _batch_major(a, nb):
    a = a.reshape(a.shape[:-2] + (a.shape[-2] // nb, nb, a.shape[-1]))
    return jnp.swapaxes(a, -3, -2)


def _trunk(x, st_a, st_b, st_p, start_pos, w, *, m):
    nb = x.shape[0]
    x = _time_major(x)
    st_a, st_b, st_p = _time_major(st_a), _time_major(st_b), _time_major(st_p)
    new_a, new_b, new_p = [], [], []
    for layer in range(DEPTH):
        i = layer // 2
        x = _ffn(x, w["norm_g"][layer, 0], w["wg"][layer, 0], w["wu"][layer, 0], w["wd"][layer, 0],
                 w["final_g"], final=False)
        if layer % 2 == 0:
            x, sa, sb = _conv_mixer(x, st_a[i], st_b[i], w["norm_g"][layer, 1], w["w_in"][i], w["caw"][i],
                                    w["cbw"][i], w["lng"][i], w["lnb"][i], w["w_out"][i], nb=nb, m=m)
            new_a.append(sa)
            new_b.append(sb)
        else:
            x, sp = _pool_mixer(x, st_p[i], w["norm_g"][layer, 1], w["wp"][i], w["psc"][i],
                                nb=nb, m=m, start_pos=start_pos)
            new_p.append(sp)
        x = _ffn(x, w["norm_g"][layer, 2], w["wg"][layer, 1], w["wu"][layer, 1], w["wd"][layer, 1],
                 w["final_g"], final=(layer == DEPTH - 1))
    return (_batch_major(x, nb), _batch_major(jnp.stack(new_a), nb), _batch_major(jnp.stack(new_b), nb),
            _batch_major(jnp.stack(new_p), nb))


def _prep_weights(norm_g, final_norm_g, w_ffn_gate, w_ffn_up, w_ffn_down, w_in_conv, conv_a_w, conv_b_w,
                  ln_b_g, ln_b_b, w_out_conv, w_pool, pool_scale):
    n_chunks = D_FF // FFN_CHUNK
    n_conv = w_in_conv.shape[0]

    def col_chunks(wt):
        wt = wt.astype(BF16).reshape(DEPTH, 2, D_MODEL, n_chunks, FFN_CHUNK)
        return wt.transpose(0, 1, 3, 2, 4)

    return {
        "norm_g": norm_g.reshape(DEPTH, 3, 1, D_MODEL),
        "final_g": final_norm_g.reshape(1, D_MODEL),
        "wg": col_chunks(w_ffn_gate),
        "wu": col_chunks(w_ffn_up),
        "wd": w_ffn_down.astype(BF16).reshape(DEPTH, 2, n_chunks, FFN_CHUNK, D_MODEL),
        "w_in": w_in_conv.astype(BF16).reshape(n_conv, D_MODEL, 5, D_MODEL).transpose(0, 2, 1, 3),
        "caw": conv_a_w,
        "cbw": conv_b_w,
        "lng": ln_b_g.reshape(n_conv, 1, D_MODEL),
        "lnb": ln_b_b.reshape(n_conv, 1, D_MODEL),
        "w_out": w_out_conv.astype(BF16).reshape(n_conv, 2, D_MODEL, D_MODEL),
        "wp": w_pool.astype(BF16),
        "psc": pool_scale.reshape(-1, 1, D_MODEL),
    }


def kernel(x_prompt, x_sample, state_conv_a, state_conv_b, state_pool, norm_g, final_norm_g, w_ffn_gate,
           w_ffn_up, w_ffn_down, w_in_conv, conv_a_w, conv_b_w, ln_b_g, ln_b_b, w_out_conv, w_pool,
           pool_scale):
    w = _prep_weights(norm_g, final_norm_g, w_ffn_gate, w_ffn_up, w_ffn_down, w_in_conv, conv_a_w, conv_b_w,
                      ln_b_g, ln_b_b, w_out_conv, w_pool, pool_scale)
    n_conv, n_pool = w_in_conv.shape[0], w_pool.shape[0]
    b = x_prompt.shape[0]
    z_a = jnp.zeros((n_conv, b, CONV_A_W - 1, D_MODEL), F32)
    z_b = jnp.zeros((n_conv, b, CONV_B_W - 1, D_MODEL), F32)
    z_p = jnp.zeros((n_pool, b, POOL_PREV, D_MODEL), F32)
    y_p, pa_a, pa_b, pa_p = _trunk(x_prompt, z_a, z_b, z_p, 0, w, m=MIX_ROWS)
    db, dt, _ = x_sample.shape
    y_s, sa_a, sa_b, sa_p = _trunk(x_sample, state_conv_a, state_conv_b, state_pool, PAST_LEN, w, m=db * dt)
    return (y_p, y_s, pa_a, sa_a, pa_b, sa_b, pa_p, sa_p)
```

```python
import functools

import jax
import jax.numpy as jnp
from jax import lax
from jax.experimental import pallas as pl
from jax.experimental.pallas import tpu as pltpu

D_MODEL = 1024
D_FF = 4 * D_MODEL
DEPTH = 4
CONV_A_W = 3
CONV_B_W = 31
POOL_WINDOWS = (2, 4, 8, 16)
POOL_G = D_MODEL // len(POOL_WINDOWS)
POOL_PREV = max(POOL_WINDOWS) - 1
PAST_LEN = 2048
RMS_EPS = 1e-6
LN_EPS = 1e-5

SUBLANES = 8
LANES = 128
MXU_N = 256
FFN_TM = 512
FFN_CHUNK = 512
MIX_ROWS = 512
CONV_OUT_BLOCK = 8
VMEM_LIMIT = 56 * 1024 * 1024

F32 = jnp.float32
BF16 = jnp.bfloat16


def _rms(x, g):
    ms = jnp.mean(x * x, axis=-1, keepdims=True)
    return x * lax.rsqrt(ms + RMS_EPS) * g


def _resident(shape):
    zeros = (0,) * len(shape)
    return pl.BlockSpec(shape, lambda *_: zeros, pipeline_mode=pl.Buffered(1))


def _ffn_body(x_ref, g_ref, wg_ref, wu_ref, wd_ref, fg_ref, o_ref, u_scr, *, final):
    u_scr[...] = _rms(x_ref[...], g_ref[...]).astype(BF16)
    for j in range(D_FF // FFN_CHUNK):
        ff = slice(j * FFN_CHUNK, (j + 1) * FFN_CHUNK)
        u = u_scr[...]
        gate = jnp.dot(u, wg_ref[:, ff], preferred_element_type=F32)
        up = jnp.dot(u, wu_ref[:, ff], preferred_element_type=F32)
        h = (gate * jax.nn.sigmoid(gate) * (0.5 * up)).astype(BF16)
        d = jnp.dot(h, wd_ref[ff, :], preferred_element_type=F32)
        if j == 0:
            o_ref[...] = x_ref[...] + d
        else:
            o_ref[...] += d
    if final:
        o_ref[...] = _rms(o_ref[...], fg_ref[...])


def _ffn(x, g, wg, wu, wd, fg, *, final):
    n = x.shape[0]
    tm = min(FFN_TM, n)
    row = pl.BlockSpec((tm, D_MODEL), lambda i: (i, 0))
    return pl.pallas_call(
        functools.partial(_ffn_body, final=final),
        grid=(n // tm,),
        in_specs=[row, _resident((1, D_MODEL)), _resident(wg.shape), _resident(wu.shape),
                  _resident(wd.shape), _resident((1, D_MODEL))],
        out_specs=row,
        out_shape=jax.ShapeDtypeStruct((n, D_MODEL), F32),
        scratch_shapes=[pltpu.VMEM((tm, D_MODEL), BF16)],
        compiler_params=pltpu.CompilerParams(dimension_semantics=("arbitrary",),
                                             vmem_limit_bytes=VMEM_LIMIT),
        name="ffn_final" if final else "ffn",
    )(x, g, wg, wu, wd, fg)


def _dwconv_block(w_ref, ext_ref, n_taps, nb, m, c0, width, emit):
    stride = nb // SUBLANES
    n_out = m // nb
    blk = min(CONV_OUT_BLOCK, n_out)
    for lane0 in range(c0, c0 + width, LANES):
        lanes = slice(lane0, lane0 + LANES)
        taps = [jnp.broadcast_to(w_ref[k:k + 1, lanes], (SUBLANES, LANES)) for k in range(n_taps)]
        for phase in range(stride):
            for i0 in range(0, n_out, blk):
                accs = [None] * blk
                for n in range(i0, i0 + blk + n_taps - 1):
                    r = (phase + stride * n) * SUBLANES
                    e = ext_ref[r:r + SUBLANES, lanes]
                    for ii in range(blk):
                        k = n - (i0 + ii)
                        if 0 <= k < n_taps:
                            term = taps[k] * e
                            accs[ii] = term if accs[ii] is None else accs[ii] + term
                for ii in range(blk):
                    emit((phase + stride * (i0 + ii)) * SUBLANES, lanes, accs[ii])


def _conv_body(x_ref, pa_ref, pb_ref, g_ref, win_ref, caw_ref, cbw_ref, lng_ref, lnb_ref, wout_ref,
               o_ref, na_ref, nb_ref, ext_a, ext_b, ya_scr, yb_scr, *, nb, m, carry):
    ha = (CONV_A_W - 1) * nb
    hb = (CONV_B_W - 1) * nb

    @pl.when(pl.program_id(0) == 0)
    def _():
        ext_a[0:ha, :] = pa_ref[...]
        ext_b[0:hb, :] = pb_ref[...]

    u = _rms(x_ref[...], g_ref[...]).astype(BF16)

    def proj(i, cols):
        return jnp.dot(u, win_ref[:, i * D_MODEL + cols.start:i * D_MODEL + cols.stop], preferred_element_type=F32)

    for c0 in range(0, D_MODEL, MXU_N):
        cols = slice(c0, c0 + MXU_N)
        ext_b[hb:hb + m, cols] = proj(3, cols) * jax.nn.sigmoid(proj(4, cols))

        def emit_b(r0, lanes, acc):
            yb_scr[r0:r0 + SUBLANES, lanes] = acc

        _dwconv_block(cbw_ref, ext_b, CONV_B_W, nb, m, c0, MXU_N, emit_b)

        ext_a[ha:ha + m, cols] = proj(2, cols) * proj(0, cols)
        ya_scr[:, cols] = proj(1, cols)

        def emit_a(r0, lanes, acc):
            ya_scr[r0:r0 + SUBLANES, lanes] = ya_scr[r0:r0 + SUBLANES, lanes] * acc

        _dwconv_block(caw_ref, ext_a, CONV_A_W, nb, m, c0, MXU_N, emit_a)

    yb = yb_scr[...]
    mu = jnp.mean(yb, axis=-1, keepdims=True)
    yc = yb - mu
    var = jnp.mean(yc * yc, axis=-1, keepdims=True)
    yb = yc * lax.rsqrt(var + LN_EPS) * lng_ref[...] + lnb_ref[...]
    yb = yb * jax.nn.sigmoid(yb)

    y = jnp.dot(ya_scr[...].astype(BF16), wout_ref[0], preferred_element_type=F32)
    y = y + jnp.dot(yb.astype(BF16), wout_ref[1], preferred_element_type=F32)
    o_ref[...] = x_ref[...] + y

    new_a = ext_a[m:m + ha, :]
    new_b = ext_b[m:m + hb, :]
    na_ref[...] = new_a
    nb_ref[...] = new_b
    if carry:
        ext_a[0:ha, :] = new_a
        ext_b[0:hb, :] = new_b


def _conv_mixer(x, prev_a, prev_b, g, w_in, caw, cbw, lng, lnb, w_out, *, nb, m):
    n = x.shape[0]
    ha = (CONV_A_W - 1) * nb
    hb = (CONV_B_W - 1) * nb
    n_tiles = n // m
    assert n_tiles == 1 or m >= hb
    body = functools.partial(_conv_body, nb=nb, m=m, carry=n_tiles > 1)
    row = pl.BlockSpec((m, D_MODEL), lambda i: (i, 0))
    return pl.pallas_call(
        body,
        grid=(n_tiles,),
        in_specs=[row, _resident((ha, D_MODEL)), _resident((hb, D_MODEL)), _resident((1, D_MODEL)),
                  _resident(w_in.shape), _resident(caw.shape), _resident(cbw.shape),
                  _resident((1, D_MODEL)), _resident((1, D_MODEL)), _resident(w_out.shape)],
        out_specs=[row, pl.BlockSpec((ha, D_MODEL), lambda i: (0, 0)),
                   pl.BlockSpec((hb, D_MODEL), lambda i: (0, 0))],
        out_shape=[jax.ShapeDtypeStruct(x.shape, F32),
                   jax.ShapeDtypeStruct((ha, D_MODEL), F32),
                   jax.ShapeDtypeStruct((hb, D_MODEL), F32)],
        scratch_shapes=[pltpu.VMEM((ha + m, D_MODEL), F32), pltpu.VMEM((hb + m, D_MODEL), F32),
                        pltpu.VMEM((m, D_MODEL), F32), pltpu.VMEM((m, D_MODEL), F32)],
        compiler_params=pltpu.CompilerParams(dimension_semantics=("arbitrary",),
                                             vmem_limit_bytes=VMEM_LIMIT),
        name="conv_mixer",
    )(x, prev_a, prev_b, g, w_in, caw, cbw, lng, lnb, w_out)


def _pool_body(x_ref, pp_ref, g_ref, wp_ref, sc_ref, o_ref, np_ref, ext, *, nb, m, start_pos, carry):
    hp = POOL_PREV * nb
    tt = m // nb
    t_idx = pl.program_id(0)

    @pl.when(t_idx == 0)
    def _():
        ext[0:hp, :] = pp_ref[...]

    x = x_ref[...]
    ext[hp:hp + m, :] = _rms(x, g_ref[...])

    log2_nb = nb.bit_length() - 1
    assert nb == 1 << log2_nb
    step = lax.shift_right_logical(lax.broadcasted_iota(jnp.int32, (m, 1), 0), log2_nb)
    pos = (start_pos + t_idx * tt + step).astype(F32)
    ys = []
    for gi, win in enumerate(POOL_WINDOWS):
        cols = slice(gi * POOL_G, (gi + 1) * POOL_G)
        s = None
        for k in range(win - 1, -1, -1):
            term = ext[hp - k * nb:hp - k * nb + m, cols]
            s = term if s is None else s + term
        cnt = jnp.minimum(jnp.float32(win), pos + 1.0)
        p = s / cnt - ext[hp:hp + m, cols]
        ys.append(jnp.dot(p.astype(BF16), wp_ref[gi], preferred_element_type=F32))
    y = jnp.concatenate(ys, axis=-1) * sc_ref[...]
    o_ref[...] = x + y

    new_p = ext[m:m + hp, :]
    np_ref[...] = new_p
    if carry:
        ext[0:hp, :] = new_p


def _pool_mixer(x, prev_p, g, wp, sc, *, nb, m, start_pos):
    n = x.shape[0]
    hp = POOL_PREV * nb
    n_tiles = n // m
    assert n_tiles == 1 or m >= hp
    body = functools.partial(_pool_body, nb=nb, m=m, start_pos=start_pos, carry=n_tiles > 1)
    row = pl.BlockSpec((m, D_MODEL), lambda i: (i, 0))
    return pl.pallas_call(
        body,
        grid=(n_tiles,),
        in_specs=[row, _resident((hp, D_MODEL)), _resident((1, D_MODEL)), _resident(wp.shape),
                  _resident((1, D_MODEL))],
        out_specs=[row, pl.BlockSpec((hp, D_MODEL), lambda i: (0, 0))],
        out_shape=[jax.ShapeDtypeStruct(x.shape, F32), jax.ShapeDtypeStruct((hp, D_MODEL), F32)],
        scratch_shapes=[pltpu.VMEM((hp + m, D_MODEL), F32)],
        compiler_params=pltpu.CompilerParams(dimension_semantics=("arbitrary",),
                                             vmem_limit_bytes=VMEM_LIMIT),
        name="pool_mixer",
    )(x, prev_p, g, wp, sc)


def _time_major(a):
    a = jnp.swapaxes(a, -3, -2)
    return a.reshape(a.shape[:-3] + (a.shape[-3] * a.shape[-2], a.shape[-1]))


def _batch_major(a, nb):
    a = a.reshape(a.shape[:-2] + (a.shape[-2] // nb, nb, a.shape[-1]))
    return jnp.swapaxes(a, -3, -2)


def _trunk(x, st_a, st_b, st_p, start_pos, w, *, m):
    nb = x.shape[0]
    x = _time_major(x)
    st_a, st_b, st_p = _time_major(st_a), _time_major(st_b), _time_major(st_p)
    new_a, new_b, new_p = [], [], []
    for layer in range(DEPTH):
        i = layer // 2
        x = _ffn(x, w["norm_g"][layer, 0], w["wg"][layer, 0], w["wu"][layer, 0], w["wd"][layer, 0],
                 w["final_g"], final=False)
        if layer % 2 == 0:
            x, sa, sb = _conv_mixer(x, st_a[i], st_b[i], w["norm_g"][layer, 1], w["w_in"][i], w["caw"][i],
                                    w["cbw"][i], w["lng"][i], w["lnb"][i], w["w_out"][i], nb=nb, m=m)
            new_a.append(sa)
            new_b.append(sb)
        else:
            x, sp = _pool_mixer(x, st_p[i], w["norm_g"][layer, 1], w["wp"][i], w["psc"][i],
                                nb=nb, m=m, start_pos=start_pos)
            new_p.append(sp)
        x = _ffn(x, w["norm_g"][layer, 2], w["wg"][layer, 1], w["wu"][layer, 1], w["wd"][layer, 1],
                 w["final_g"], final=(layer == DEPTH - 1))
    return (_batch_major(x, nb), _batch_major(jnp.stack(new_a), nb), _batch_major(jnp.stack(new_b), nb),
            _batch_major(jnp.stack(new_p), nb))


def _prep_weights(norm_g, final_norm_g, w_ffn_gate, w_ffn_up, w_ffn_down, w_in_conv, conv_a_w, conv_b_w,
                  ln_b_g, ln_b_b, w_out_conv, w_pool, pool_scale):
    n_conv = w_in_conv.shape[0]
    return {
        "norm_g": norm_g.reshape(DEPTH, 3, 1, D_MODEL),
        "final_g": final_norm_g.reshape(1, D_MODEL),
        "wg": w_ffn_gate.astype(BF16),
        "wu": w_ffn_up.astype(BF16),
        "wd": w_ffn_down.astype(BF16),
        "w_in": w_in_conv.astype(BF16),
        "caw": conv_a_w,
        "cbw": conv_b_w,
        "lng": ln_b_g.reshape(n_conv, 1, D_MODEL),
        "lnb": ln_b_b.reshape(n_conv, 1, D_MODEL),
        "w_out": w_out_conv.astype(BF16).reshape(n_conv, 2, D_MODEL, D_MODEL),
        "wp": w_pool.astype(BF16),
        "psc": pool_scale.reshape(-1, 1, D_MODEL),
    }


def kernel(x_prompt, x_sample, state_conv_a, state_conv_b, state_pool, norm_g, final_norm_g, w_ffn_gate,
           w_ffn_up, w_ffn_down, w_in_conv, conv_a_w, conv_b_w, ln_b_g, ln_b_b, w_out_conv, w_pool,
           pool_scale):
    w = _prep_weights(norm_g, final_norm_g, w_ffn_gate, w_ffn_up, w_ffn_down, w_in_conv, conv_a_w, conv_b_w,
                      ln_b_g, ln_b_b, w_out_conv, w_pool, pool_scale)
    n_conv, n_pool = w_in_conv.shape[0], w_pool.shape[0]
    b = x_prompt.shape[0]
    z_a = jnp.zeros((n_conv, b, CONV_A_W - 1, D_MODEL), F32)
    z_b = jnp.zeros((n_conv, b, CONV_B_W - 1, D_MODEL), F32)
    z_p = jnp.zeros((n_pool, b, POOL_PREV, D_MODEL), F32)
    y_p, pa_a, pa_b, pa_p = _trunk(x_prompt, z_a, z_b, z_p, 0, w, m=MIX_ROWS)
    db, dt, _ = x_sample.shape
    y_s, sa_a, sa_b, sa_p = _trunk(x_sample, state_conv_a, state_conv_b, state_pool, PAST_LEN, w, m=db * dt)
    return (y_p, y_s, pa_a, sa_a, pa_b, sa_b, pa_p, sa_p)
```

```python
import functools

import jax
import jax.numpy as jnp
from jax import lax
from jax.experimental import pallas as pl
from jax.experimental.pallas import tpu as pltpu

D_MODEL = 1024
D_FF = 4 * D_MODEL
DEPTH = 4
CONV_A_W = 3
CONV_B_W = 31
POOL_WINDOWS = (2, 4, 8, 16)
POOL_G = D_MODEL // len(POOL_WINDOWS)
POOL_PREV = max(POOL_WINDOWS) - 1
PAST_LEN = 2048
RMS_EPS = 1e-6
LN_EPS = 1e-5

SUBLANES = 8
BF16_ROWS = 16
CAST_BLOCK_BYTES = 8 * 1024 * 1024
LANES = 128
MXU_N = 256
FFN_TM = 512
FFN_CHUNK = 512
CONV_ROWS = 512
POOL_ROWS = 1024
CONV_OUT_BLOCK = 8
VMEM_LIMIT = 56 * 1024 * 1024

F32 = jnp.float32
BF16 = jnp.bfloat16


def _rms(x, g):
    ms = jnp.mean(x * x, axis=-1, keepdims=True)
    return x * lax.rsqrt(ms + RMS_EPS) * g


def _resident(shape):
    zeros = (0,) * len(shape)
    return pl.BlockSpec(shape, lambda *_: zeros, pipeline_mode=pl.Buffered(1))


def _cast_body(w_ref, o_ref):
    o_ref[...] = w_ref[...].astype(BF16)


def _to_bf16(w):
    shape = w.shape
    cols = shape[-1]
    w2 = w.reshape(-1, cols)
    rows = w2.shape[0]
    br = min(rows, 1 << ((CAST_BLOCK_BYTES // (cols * 4)).bit_length() - 1))
    assert rows % br == 0 and br % BF16_ROWS == 0
    blk = pl.BlockSpec((br, cols), lambda i: (i, 0))
    out = pl.pallas_call(
        _cast_body,
        grid=(rows // br,),
        in_specs=[blk],
        out_specs=blk,
        out_shape=jax.ShapeDtypeStruct(w2.shape, BF16),
        compiler_params=pltpu.CompilerParams(dimension_semantics=("arbitrary",),
                                             vmem_limit_bytes=VMEM_LIMIT),
        name="cast_bf16",
    )(w2)
    return out.reshape(shape)


def _ffn_body(x_ref, g_ref, wg_ref, wu_ref, wd_ref, fg_ref, o_ref, u_scr, *scratch, final, nb_in, nb_out):
    scratch = list(scratch)
    if nb_in:
        x_tm = scratch.pop(0)
        for t in range(x_ref.shape[1]):
            x_tm[t * nb_in:(t + 1) * nb_in, :] = x_ref[:, t, :]
    else:
        x_tm = x_ref
    acc = scratch.pop(0) if nb_out else o_ref

    u_scr[...] = _rms(x_tm[...], g_ref[...]).astype(BF16)
    for j in range(D_FF // FFN_CHUNK):
        ff = slice(j * FFN_CHUNK, (j + 1) * FFN_CHUNK)
        u = u_scr[...]
        gate = jnp.dot(u, wg_ref[:, ff], preferred_element_type=F32)
        up = jnp.dot(u, wu_ref[:, ff], preferred_element_type=F32)
        h = (gate * jax.nn.sigmoid(gate) * (0.5 * up)).astype(BF16)
        d = jnp.dot(h, wd_ref[ff, :], preferred_element_type=F32)
        if j == 0:
            acc[...] = x_tm[...] + d
        else:
            acc[...] += d
    if final:
        acc[...] = _rms(acc[...], fg_ref[...])
    if nb_out:
        for t in range(o_ref.shape[1]):
            o_ref[:, t, :] = acc[t * nb_out:(t + 1) * nb_out, :]


def _ffn(x, g, wg, wu, wd, fg, *, final, nb_in=None, nb_out=None):
    n = x.shape[0] * x.shape[1] if nb_in else x.shape[0]
    tm = min(FFN_TM, n)
    row = pl.BlockSpec((tm, D_MODEL), lambda i: (i, 0))

    def batch_major(nb):
        return pl.BlockSpec((nb, tm // nb, D_MODEL), lambda i: (0, i, 0))

    scratch = [pltpu.VMEM((tm, D_MODEL), BF16)]
    scratch += [pltpu.VMEM((tm, D_MODEL), F32)] * (bool(nb_in) + bool(nb_out))
    out_shape = (nb_out, n // nb_out, D_MODEL) if nb_out else (n, D_MODEL)
    return pl.pallas_call(
        functools.partial(_ffn_body, final=final, nb_in=nb_in, nb_out=nb_out),
        grid=(n // tm,),
        in_specs=[batch_major(nb_in) if nb_in else row, _resident((1, D_MODEL)), _resident(wg.shape),
                  _resident(wu.shape), _resident(wd.shape), _resident((1, D_MODEL))],
        out_specs=batch_major(nb_out) if nb_out else row,
        out_shape=jax.ShapeDtypeStruct(out_shape, F32),
        scratch_shapes=scratch,
        compiler_params=pltpu.CompilerParams(dimension_semantics=("arbitrary",),
                                             vmem_limit_bytes=VMEM_LIMIT),
        name="ffn_final" if final else "ffn",
    )(x, g, wg, wu, wd, fg)


def _dwconv_block(w_ref, ext_ref, n_taps, nb, m, c0, width, emit):
    stride = nb // SUBLANES
    n_out = m // nb
    blk = min(CONV_OUT_BLOCK, n_out)
    for lane0 in range(c0, c0 + width, LANES):
        lanes = slice(lane0, lane0 + LANES)
        taps = [jnp.broadcast_to(w_ref[k:k + 1, lanes], (SUBLANES, LANES)) for k in range(n_taps)]
        for phase in range(stride):
            for i0 in range(0, n_out, blk):
                accs = [None] * blk
                for n in range(i0, i0 + blk + n_taps - 1):
                    r = (phase + stride * n) * SUBLANES
                    e = ext_ref[r:r + SUBLANES, lanes]
                    for ii in range(blk):
                        k = n - (i0 + ii)
                        if 0 <= k < n_taps:
                            term = taps[k] * e
                            accs[ii] = term if accs[ii] is None else accs[ii] + term
                for ii in range(blk):
                    emit((phase + stride * (i0 + ii)) * SUBLANES, lanes, accs[ii])


def _conv_body(x_ref, pa_ref, pb_ref, g_ref, win_ref, caw_ref, cbw_ref, lng_ref, lnb_ref, wout_ref,
               o_ref, na_ref, nb_ref, ext_a, ext_b, ya_scr, yb_scr, *, nb, m, carry):
    ha = (CONV_A_W - 1) * nb
    hb = (CONV_B_W - 1) * nb

    @pl.when(pl.program_id(0) == 0)
    def _():
        ext_a[0:ha, :] = pa_ref[...]
        ext_b[0:hb, :] = pb_ref[...]

    u = _rms(x_ref[...], g_ref[...]).astype(BF16)

    def proj(i, cols):
        return jnp.dot(u, win_ref[:, i * D_MODEL + cols.start:i * D_MODEL + cols.stop], preferred_element_type=F32)

    for c0 in range(0, D_MODEL, MXU_N):
        cols = slice(c0, c0 + MXU_N)
        ext_b[hb:hb + m, cols] = proj(3, cols) * jax.nn.sigmoid(proj(4, cols))

        def emit_b(r0, lanes, acc):
            yb_scr[r0:r0 + SUBLANES, lanes] = acc

        _dwconv_block(cbw_ref, ext_b, CONV_B_W, nb, m, c0, MXU_N, emit_b)

        ext_a[ha:ha + m, cols] = proj(2, cols) * proj(0, cols)
        ya_scr[:, cols] = proj(1, cols)

        def emit_a(r0, lanes, acc):
            ya_scr[r0:r0 + SUBLANES, lanes] = ya_scr[r0:r0 + SUBLANES, lanes] * acc

        _dwconv_block(caw_ref, ext_a, CONV_A_W, nb, m, c0, MXU_N, emit_a)

    yb = yb_scr[...]
    mu = jnp.mean(yb, axis=-1, keepdims=True)
    yc = yb - mu
    var = jnp.mean(yc * yc, axis=-1, keepdims=True)
    yb = yc * lax.rsqrt(var + LN_EPS) * lng_ref[...] + lnb_ref[...]
    yb = yb * jax.nn.sigmoid(yb)

    y = jnp.dot(ya_scr[...].astype(BF16), wout_ref[0], preferred_element_type=F32)
    y = y + jnp.dot(yb.astype(BF16), wout_ref[1], preferred_element_type=F32)
    o_ref[...] = x_ref[...] + y

    new_a = ext_a[m:m + ha, :]
    new_b = ext_b[m:m + hb, :]
    na_ref[...] = new_a
    nb_ref[...] = new_b
    if carry:
        ext_a[0:ha, :] = new_a
        ext_b[0:hb, :] = new_b


def _conv_mixer(x, prev_a, prev_b, g, w_in, caw, cbw, lng, lnb, w_out, *, nb, m):
    n = x.shape[0]
    ha = (CONV_A_W - 1) * nb
    hb = (CONV_B_W - 1) * nb
    n_tiles = n // m
    assert n_tiles == 1 or m >= hb
    body = functools.partial(_conv_body, nb=nb, m=m, carry=n_tiles > 1)
    row = pl.BlockSpec((m, D_MODEL), lambda i: (i, 0))
    return pl.pallas_call(
        body,
        grid=(n_tiles,),
        in_specs=[row, _resident((ha, D_MODEL)), _resident((hb, D_MODEL)), _resident((1, D_MODEL)),
                  _resident(w_in.shape), _resident(caw.shape), _resident(cbw.shape),
                  _resident((1, D_MODEL)), _resident((1, D_MODEL)), _resident(w_out.shape)],
        out_specs=[row, pl.BlockSpec((ha, D_MODEL), lambda i: (0, 0)),
                   pl.BlockSpec((hb, D_MODEL), lambda i: (0, 0))],
        out_shape=[jax.ShapeDtypeStruct(x.shape, F32),
                   jax.ShapeDtypeStruct((ha, D_MODEL), F32),
                   jax.ShapeDtypeStruct((hb, D_MODEL), F32)],
        scratch_shapes=[pltpu.VMEM((ha + m, D_MODEL), F32), pltpu.VMEM((hb + m, D_MODEL), F32),
                        pltpu.VMEM((m, D_MODEL), F32), pltpu.VMEM((m, D_MODEL), F32)],
        compiler_params=pltpu.CompilerParams(dimension_semantics=("arbitrary",),
                                             vmem_limit_bytes=VMEM_LIMIT),
        name="conv_mixer",
    )(x, prev_a, prev_b, g, w_in, caw, cbw, lng, lnb, w_out)


def _pool_body(x_ref, pp_ref, g_ref, wp_ref, sc_ref, o_ref, np_ref, ext, *, nb, m, start_pos, carry):
    hp = POOL_PREV * nb
    tt = m // nb
    t_idx = pl.program_id(0)

    @pl.when(t_idx == 0)
    def _():
        ext[0:hp, :] = pp_ref[...]

    x = x_ref[...]
    ext[hp:hp + m, :] = _rms(x, g_ref[...])

    log2_nb = nb.bit_length() - 1
    assert nb == 1 << log2_nb
    step = lax.shift_right_logical(lax.broadcasted_iota(jnp.int32, (m, 1), 0), log2_nb)
    pos = (start_pos + t_idx * tt + step).astype(F32)
    ys = []
    for gi, win in enumerate(POOL_WINDOWS):
        cols = slice(gi * POOL_G, (gi + 1) * POOL_G)
        s = None
        for k in range(win - 1, -1, -1):
            term = ext[hp - k * nb:hp - k * nb + m, cols]
            s = term if s is None else s + term
        cnt = jnp.minimum(jnp.float32(win), pos + 1.0)
        p = s / cnt - ext[hp:hp + m, cols]
        ys.append(jnp.dot(p.astype(BF16), wp_ref[gi], preferred_element_type=F32))
    y = jnp.concatenate(ys, axis=-1) * sc_ref[...]
    o_ref[...] = x + y

    new_p = ext[m:m + hp, :]
    np_ref[...] = new_p
    if carry:
        ext[0:hp, :] = new_p


def _pool_mixer(x, prev_p, g, wp, sc, *, nb, m, start_pos):
    n = x.shape[0]
    hp = POOL_PREV * nb
    n_tiles = n // m
    assert n_tiles == 1 or m >= hp
    body = functools.partial(_pool_body, nb=nb, m=m, start_pos=start_pos, carry=n_tiles > 1)
    row = pl.BlockSpec((m, D_MODEL), lambda i: (i, 0))
    return pl.pallas_call(
        body,
        grid=(n_tiles,),
        in_specs=[row, _resident((hp, D_MODEL)), _resident((1, D_MODEL)), _resident(wp.shape),
                  _resident((1, D_MODEL))],
        out_specs=[row, pl.BlockSpec((hp, D_MODEL), lambda i: (0, 0))],
        out_shape=[jax.ShapeDtypeStruct(x.shape, F32), jax.ShapeDtypeStruct((hp, D_MODEL), F32)],
        scratch_shapes=[pltpu.VMEM((hp + m, D_MODEL), F32)],
        compiler_params=pltpu.CompilerParams(dimension_semantics=("arbitrary",),
                                             vmem_limit_bytes=VMEM_LIMIT),
        name="pool_mixer",
    )(x, prev_p, g, wp, sc)


def _time_major(a):
    a = jnp.swapaxes(a, -3, -2)
    return a.reshape(a.shape[:-3] + (a.shape[-3] * a.shape[-2], a.shape[-1]))


def _batch_major(a, nb):
    a = a.reshape(a.shape[:-2] + (a.shape[-2] // nb, nb, a.shape[-1]))
    return jnp.swapaxes(a, -3, -2)


def _trunk(x, st_a, st_b, st_p, start_pos, w, *, m_conv, m_pool):
    nb = x.shape[0]
    st_a, st_b, st_p = _time_major(st_a), _time_major(st_b), _time_major(st_p)
    new_a, new_b, new_p = [], [], []
    for layer in range(DEPTH):
        i = layer // 2
        x = _ffn(x, w["norm_g"][layer, 0], w["wg"][layer, 0], w["wu"][layer, 0], w["wd"][layer, 0],
                 w["final_g"], final=False, nb_in=nb if layer == 0 else None)
        if layer % 2 == 0:
            x, sa, sb = _conv_mixer(x, st_a[i], st_b[i], w["norm_g"][layer, 1], w["w_in"][i], w["caw"][i],
                                    w["cbw"][i], w["lng"][i], w["lnb"][i], w["w_out"][i], nb=nb, m=m_conv)
            new_a.append(sa)
            new_b.append(sb)
        else:
            x, sp = _pool_mixer(x, st_p[i], w["norm_g"][layer, 1], w["wp"][i], w["psc"][i],
                                nb=nb, m=m_pool, start_pos=start_pos)
            new_p.append(sp)
        last = layer == DEPTH - 1
        x = _ffn(x, w["norm_g"][layer, 2], w["wg"][layer, 1], w["wu"][layer, 1], w["wd"][layer, 1],
                 w["final_g"], final=last, nb_out=nb if last else None)
    return (x, _batch_major(jnp.stack(new_a), nb), _batch_major(jnp.stack(new_b), nb),
            _batch_major(jnp.stack(new_p), nb))


def _prep_weights(norm_g, final_norm_g, w_ffn_gate, w_ffn_up, w_ffn_down, w_in_conv, conv_a_w, conv_b_w,
                  ln_b_g, ln_b_b, w_out_conv, w_pool, pool_scale):
    n_conv = w_in_conv.shape[0]
    return {
        "norm_g": norm_g.reshape(DEPTH, 3, 1, D_MODEL),
        "final_g": final_norm_g.reshape(1, D_MODEL),
        "wg": _to_bf16(w_ffn_gate),
        "wu": _to_bf16(w_ffn_up),
        "wd": _to_bf16(w_ffn_down),
        "w_in": _to_bf16(w_in_conv),
        "caw": conv_a_w,
        "cbw": conv_b_w,
        "lng": ln_b_g.reshape(n_conv, 1, D_MODEL),
        "lnb": ln_b_b.reshape(n_conv, 1, D_MODEL),
        "w_out": _to_bf16(w_out_conv).reshape(n_conv, 2, D_MODEL, D_MODEL),
        "wp": _to_bf16(w_pool),
        "psc": pool_scale.reshape(-1, 1, D_MODEL),
    }


def kernel(x_prompt, x_sample, state_conv_a, state_conv_b, state_pool, norm_g, final_norm_g, w_ffn_gate,
           w_ffn_up, w_ffn_down, w_in_conv, conv_a_w, conv_b_w, ln_b_g, ln_b_b, w_out_conv, w_pool,
           pool_scale):
    w = _prep_weights(norm_g, final_norm_g, w_ffn_gate, w_ffn_up, w_ffn_down, w_in_conv, conv_a_w, conv_b_w,
                      ln_b_g, ln_b_b, w_out_conv, w_pool, pool_scale)
    n_conv, n_pool = w_in_conv.shape[0], w_pool.shape[0]
    b = x_prompt.shape[0]
    z_a = jnp.zeros((n_conv, b, CONV_A_W - 1, D_MODEL), F32)
    z_b = jnp.zeros((n_conv, b, CONV_B_W - 1, D_MODEL), F32)
    z_p = jnp.zeros((n_pool, b, POOL_PREV, D_MODEL), F32)
    y_p, pa_a, pa_b, pa_p = _trunk(x_prompt, z_a, z_b, z_p, 0, w, m_conv=CONV_ROWS, m_pool=POOL_ROWS)
    db, dt, _ = x_sample.shape
    y_s, sa_a, sa_b, sa_p = _trunk(x_sample, state_conv_a, state_conv_b, state_pool, PAST_LEN, w,
                                   m_conv=db * dt, m_pool=db * dt)
    return (y_p, y_s, pa_a, sa_a, pa_b, sa_b, pa_p, sa_p)
```

```python
import functools

import jax
import jax.numpy as jnp
from jax import lax
from jax.experimental import pallas as pl
from jax.experimental.pallas import tpu as pltpu

D_MODEL = 1024
D_FF = 4 * D_MODEL
DEPTH = 4
CONV_A_W = 3
CONV_B_W = 31
POOL_WINDOWS = (2, 4, 8, 16)
POOL_G = D_MODEL // len(POOL_WINDOWS)
POOL_PREV = max(POOL_WINDOWS) - 1
PAST_LEN = 2048
RMS_EPS = 1e-6
LN_EPS = 1e-5

SUBLANES = 8
BF16_ROWS = 16
CAST_BLOCK_BYTES = 8 * 1024 * 1024
LANES = 128
MXU_N = 256
FFN_TM = 512
FFN_CHUNK = 256
CONV_ROWS = 512
POOL_ROWS = 1024
CONV_OUT_BLOCK = 8
VMEM_LIMIT = 56 * 1024 * 1024

F32 = jnp.float32
BF16 = jnp.bfloat16


def _rms(x, g):
    ms = jnp.mean(x * x, axis=-1, keepdims=True)
    return x * lax.rsqrt(ms + RMS_EPS) * g


def _arr(arg):
    return arg[0] if isinstance(arg, tuple) else arg


def _resident(arg):
    arr, idx = arg if isinstance(arg, tuple) else (arg, ())
    rest = arr.shape[len(idx):]
    index = tuple(idx) + (0,) * len(rest)
    return pl.BlockSpec((None,) * len(idx) + rest, lambda *_: index, pipeline_mode=pl.Buffered(1))


def _cast_body(w_ref, o_ref):
    o_ref[...] = w_ref[...].astype(BF16)


def _to_bf16(w):
    shape = w.shape
    cols = shape[-1]
    w2 = w.reshape(-1, cols)
    rows = w2.shape[0]
    br = min(rows, 1 << ((CAST_BLOCK_BYTES // (cols * 4)).bit_length() - 1))
    assert rows % br == 0 and br % BF16_ROWS == 0
    blk = pl.BlockSpec((br, cols), lambda i: (i, 0))
    out = pl.pallas_call(
        _cast_body,
        grid=(rows // br,),
        in_specs=[blk],
        out_specs=blk,
        out_shape=jax.ShapeDtypeStruct(w2.shape, BF16),
        compiler_params=pltpu.CompilerParams(dimension_semantics=("arbitrary",),
                                             vmem_limit_bytes=VMEM_LIMIT),
        name="cast_bf16",
    )(w2)
    return out.reshape(shape)


def _ffn_body(x_ref, g_ref, wg_ref, wu_ref, wd_ref, fg_ref, o_ref, u_scr, *scratch, final, nb_in, nb_out):
    scratch = list(scratch)
    if nb_in:
        x_tm = scratch.pop(0)
        for t in range(x_ref.shape[1]):
            x_tm[t * nb_in:(t + 1) * nb_in, :] = x_ref[:, t, :]
    else:
        x_tm = x_ref
    acc = scratch.pop(0) if nb_out else o_ref

    u_scr[...] = _rms(x_tm[...], g_ref[...]).astype(BF16)
    for j in range(D_FF // FFN_CHUNK):
        ff = slice(j * FFN_CHUNK, (j + 1) * FFN_CHUNK)
        u = u_scr[...]
        gate = jnp.dot(u, wg_ref[:, ff], preferred_element_type=F32)
        up = jnp.dot(u, wu_ref[:, ff], preferred_element_type=F32)
        h = (gate * jax.nn.sigmoid(gate) * (0.5 * up)).astype(BF16)
        d = jnp.dot(h, wd_ref[ff, :], preferred_element_type=F32)
        if j == 0:
            acc[...] = x_tm[...] + d
        else:
            acc[...] += d
    if final:
        acc[...] = _rms(acc[...], fg_ref[...])
    if nb_out:
        for t in range(o_ref.shape[1]):
            o_ref[:, t, :] = acc[t * nb_out:(t + 1) * nb_out, :]


def _ffn(x, g, wg, wu, wd, fg, *, final, nb_in=None, nb_out=None):
    n = x.shape[0] * x.shape[1] if nb_in else x.shape[0]
    tm = min(FFN_TM, n)
    row = pl.BlockSpec((tm, D_MODEL), lambda i: (i, 0))

    def batch_major(nb):
        return pl.BlockSpec((nb, tm // nb, D_MODEL), lambda i: (0, i, 0))

    scratch = [pltpu.VMEM((tm, D_MODEL), BF16)]
    scratch += [pltpu.VMEM((tm, D_MODEL), F32)] * (bool(nb_in) + bool(nb_out))
    out_shape = (nb_out, n // nb_out, D_MODEL) if nb_out else (n, D_MODEL)
    return pl.pallas_call(
        functools.partial(_ffn_body, final=final, nb_in=nb_in, nb_out=nb_out),
        grid=(n // tm,),
        in_specs=[batch_major(nb_in) if nb_in else row] + [_resident(a) for a in (g, wg, wu, wd, fg)],
        out_specs=batch_major(nb_out) if nb_out else row,
        out_shape=jax.ShapeDtypeStruct(out_shape, F32),
        scratch_shapes=scratch,
        compiler_params=pltpu.CompilerParams(dimension_semantics=("arbitrary",),
                                             vmem_limit_bytes=VMEM_LIMIT),
        name="ffn_final" if final else "ffn",
    )(x, *[_arr(a) for a in (g, wg, wu, wd, fg)])


def _dwconv_block(w_ref, ext_ref, n_taps, nb, m, c0, width, emit):
    stride = nb // SUBLANES
    n_out = m // nb
    blk = min(CONV_OUT_BLOCK, n_out)
    for lane0 in range(c0, c0 + width, LANES):
        lanes = slice(lane0, lane0 + LANES)
        taps = [jnp.broadcast_to(w_ref[k:k + 1, lanes], (SUBLANES, LANES)) for k in range(n_taps)]
        for phase in range(stride):
            for i0 in range(0, n_out, blk):
                accs = [None] * blk
                for n in range(i0, i0 + blk + n_taps - 1):
                    r = (phase + stride * n) * SUBLANES
                    e = ext_ref[r:r + SUBLANES, lanes]
                    for ii in range(blk):
                        k = n - (i0 + ii)
                        if 0 <= k < n_taps:
                            term = taps[k] * e
                            accs[ii] = term if accs[ii] is None else accs[ii] + term
                for ii in range(blk):
                    emit((phase + stride * (i0 + ii)) * SUBLANES, lanes, accs[ii])


def _conv_body(x_ref, pa_ref, pb_ref, g_ref, win_ref, caw_ref, cbw_ref, lng_ref, lnb_ref, wout_ref,
               o_ref, na_ref, nb_ref, ext_a, ext_b, ya_scr, yb_scr, *, nb, m, carry):
    ha = (CONV_A_W - 1) * nb
    hb = (CONV_B_W - 1) * nb

    @pl.when(pl.program_id(0) == 0)
    def _():
        ext_a[0:ha, :] = pa_ref[...]
        ext_b[0:hb, :] = pb_ref[...]

    u = _rms(x_ref[...], g_ref[...]).astype(BF16)

    def proj(i, cols):
        return jnp.dot(u, win_ref[:, i * D_MODEL + cols.start:i * D_MODEL + cols.stop], preferred_element_type=F32)

    for c0 in range(0, D_MODEL, MXU_N):
        cols = slice(c0, c0 + MXU_N)
        ext_b[hb:hb + m, cols] = proj(3, cols) * jax.nn.sigmoid(proj(4, cols))

        def emit_b(r0, lanes, acc):
            yb_scr[r0:r0 + SUBLANES, lanes] = acc

        _dwconv_block(cbw_ref, ext_b, CONV_B_W, nb, m, c0, MXU_N, emit_b)

        ext_a[ha:ha + m, cols] = proj(2, cols) * proj(0, cols)
        ya_scr[:, cols] = proj(1, cols)

        def emit_a(r0, lanes, acc):
            ya_scr[r0:r0 + SUBLANES, lanes] = ya_scr[r0:r0 + SUBLANES, lanes] * acc

        _dwconv_block(caw_ref, ext_a, CONV_A_W, nb, m, c0, MXU_N, emit_a)

    yb = yb_scr[...]
    mu = jnp.mean(yb, axis=-1, keepdims=True)
    yc = yb - mu
    var = jnp.mean(yc * yc, axis=-1, keepdims=True)
    yb = yc * lax.rsqrt(var + LN_EPS) * lng_ref[...] + lnb_ref[...]
    yb = yb * jax.nn.sigmoid(yb)

    y = jnp.dot(ya_scr[...].astype(BF16), wout_ref[0], preferred_element_type=F32)
    y = y + jnp.dot(yb.astype(BF16), wout_ref[1], preferred_element_type=F32)
    o_ref[...] = x_ref[...] + y

    new_a = ext_a[m:m + ha, :]
    new_b = ext_b[m:m + hb, :]
    na_ref[...] = new_a
    nb_ref[...] = new_b
    if carry:
        ext_a[0:ha, :] = new_a
        ext_b[0:hb, :] = new_b


def _conv_mixer(x, prev_a, prev_b, g, w_in, caw, cbw, lng, lnb, w_out, *, nb, m):
    n = x.shape[0]
    ha = (CONV_A_W - 1) * nb
    hb = (CONV_B_W - 1) * nb
    n_tiles = n // m
    assert n_tiles == 1 or m >= hb
    body = functools.partial(_conv_body, nb=nb, m=m, carry=n_tiles > 1)
    row = pl.BlockSpec((m, D_MODEL), lambda i: (i, 0))
    return pl.pallas_call(
        body,
        grid=(n_tiles,),
        in_specs=[row] + [_resident(a) for a in (prev_a, prev_b, g, w_in, caw, cbw, lng, lnb, w_out)],
        out_specs=[row, pl.BlockSpec((ha, D_MODEL), lambda i: (0, 0)),
                   pl.BlockSpec((hb, D_MODEL), lambda i: (0, 0))],
        out_shape=[jax.ShapeDtypeStruct(x.shape, F32),
                   jax.ShapeDtypeStruct((ha, D_MODEL), F32),
                   jax.ShapeDtypeStruct((hb, D_MODEL), F32)],
        scratch_shapes=[pltpu.VMEM((ha + m, D_MODEL), F32), pltpu.VMEM((hb + m, D_MODEL), F32),
                        pltpu.VMEM((m, D_MODEL), F32), pltpu.VMEM((m, D_MODEL), F32)],
        compiler_params=pltpu.CompilerParams(dimension_semantics=("arbitrary",),
                                             vmem_limit_bytes=VMEM_LIMIT),
        name="conv_mixer",
    )(x, *[_arr(a) for a in (prev_a, prev_b, g, w_in, caw, cbw, lng, lnb, w_out)])


def _pool_body(x_ref, pp_ref, g_ref, wp_ref, sc_ref, o_ref, np_ref, ext, *, nb, m, start_pos, carry):
    hp = POOL_PREV * nb
    tt = m // nb
    t_idx = pl.program_id(0)

    @pl.when(t_idx == 0)
    def _():
        ext[0:hp, :] = pp_ref[...]

    x = x_ref[...]
    ext[hp:hp + m, :] = _rms(x, g_ref[...])

    log2_nb = nb.bit_length() - 1
    assert nb == 1 << log2_nb
    step = lax.shift_right_logical(lax.broadcasted_iota(jnp.int32, (m, 1), 0), log2_nb)
    pos = (start_pos + t_idx * tt + step).astype(F32)
    ys = []
    for gi, win in enumerate(POOL_WINDOWS):
        cols = slice(gi * POOL_G, (gi + 1) * POOL_G)
        s = None
        for k in range(win - 1, -1, -1):
            term = ext[hp - k * nb:hp - k * nb + m, cols]
            s = term if s is None else s + term
        cnt = jnp.minimum(jnp.float32(win), pos + 1.0)
        p = s / cnt - ext[hp:hp + m, cols]
        ys.append(jnp.dot(p.astype(BF16), wp_ref[gi], preferred_element_type=F32))
    y = jnp.concatenate(ys, axis=-1) * sc_ref[...]
    o_ref[...] = x + y

    new_p = ext[m:m + hp, :]
    np_ref[...] = new_p
    if carry:
        ext[0:hp, :] = new_p


def _pool_mixer(x, prev_p, g, wp, sc, *, nb, m, start_pos):
    n = x.shape[0]
    hp = POOL_PREV * nb
    n_tiles = n // m
    assert n_tiles == 1 or m >= hp
    body = functools.partial(_pool_body, nb=nb, m=m, start_pos=start_pos, carry=n_tiles > 1)
    row = pl.BlockSpec((m, D_MODEL), lambda i: (i, 0))
    return pl.pallas_call(
        body,
        grid=(n_tiles,),
        in_specs=[row] + [_resident(a) for a in (prev_p, g, wp, sc)],
        out_specs=[row, pl.BlockSpec((hp, D_MODEL), lambda i: (0, 0))],
        out_shape=[jax.ShapeDtypeStruct(x.shape, F32), jax.ShapeDtypeStruct((hp, D_MODEL), F32)],
        scratch_shapes=[pltpu.VMEM((hp + m, D_MODEL), F32)],
        compiler_params=pltpu.CompilerParams(dimension_semantics=("arbitrary",),
                                             vmem_limit_bytes=VMEM_LIMIT),
        name="pool_mixer",
    )(x, *[_arr(a) for a in (prev_p, g, wp, sc)])


def _time_major(a):
    a = jnp.swapaxes(a, -3, -2)
    return a.reshape(a.shape[:-3] + (a.shape[-3] * a.shape[-2], a.shape[-1]))


def _batch_major(a, nb):
    a = a.reshape(a.shape[:-2] + (a.shape[-2] // nb, nb, a.shape[-1]))
    return jnp.swapaxes(a, -3, -2)


def _trunk(x, st_a, st_b, st_p, start_pos, w, *, m_conv, m_pool):
    nb = x.shape[0]
    st_a, st_b, st_p = _time_major(st_a), _time_major(st_b), _time_major(st_p)
    new_a, new_b, new_p = [], [], []
    def pick(name, *idx):
        return (w[name], idx)

    for layer in range(DEPTH):
        i = layer // 2
        x = _ffn(x, pick("norm_g", layer, 0), pick("wg", layer, 0), pick("wu", layer, 0), pick("wd", layer, 0),
                 w["final_g"], final=False, nb_in=nb if layer == 0 else None)
        if layer % 2 == 0:
            x, sa, sb = _conv_mixer(x, (st_a, (i,)), (st_b, (i,)), pick("norm_g", layer, 1), pick("w_in", i),
                                    pick("caw", i), pick("cbw", i), pick("lng", i), pick("lnb", i),
                                    pick("w_out", i), nb=nb, m=m_conv)
            new_a.append(sa)
            new_b.append(sb)
        else:
            x, sp = _pool_mixer(x, (st_p, (i,)), pick("norm_g", layer, 1), pick("wp", i), pick("psc", i),
                                nb=nb, m=m_pool, start_pos=start_pos)
            new_p.append(sp)
        last = layer == DEPTH - 1
        x = _ffn(x, pick("norm_g", layer, 2), pick("wg", layer, 1), pick("wu", layer, 1), pick("wd", layer, 1),
                 w["final_g"], final=last, nb_out=nb if last else None)
    return (x, _batch_major(jnp.stack(new_a), nb), _batch_major(jnp.stack(new_b), nb),
            _batch_major(jnp.stack(new_p), nb))


def _prep_weights(norm_g, final_norm_g, w_ffn_gate, w_ffn_up, w_ffn_down, w_in_conv, conv_a_w, conv_b_w,
                  ln_b_g, ln_b_b, w_out_conv, w_pool, pool_scale):
    n_conv = w_in_conv.shape[0]
    return {
        "norm_g": norm_g.reshape(DEPTH, 3, 1, D_MODEL),
        "final_g": final_norm_g.reshape(1, D_MODEL),
        "wg": _to_bf16(w_ffn_gate),
        "wu": _to_bf16(w_ffn_up),
        "wd": _to_bf16(w_ffn_down),
        "w_in": _to_bf16(w_in_conv),
        "caw": conv_a_w,
        "cbw": conv_b_w,
        "lng": ln_b_g.reshape(n_conv, 1, D_MODEL),
        "lnb": ln_b_b.reshape(n_conv, 1, D_MODEL),
        "w_out": _to_bf16(w_out_conv).reshape(n_conv, 2, D_MODEL, D_MODEL),
        "wp": _to_bf16(w_pool),
        "psc": pool_scale.reshape(-1, 1, D_MODEL),
    }


def kernel(x_prompt, x_sample, state_conv_a, state_conv_b, state_pool, norm_g, final_norm_g, w_ffn_gate,
           w_ffn_up, w_ffn_down, w_in_conv, conv_a_w, conv_b_w, ln_b_g, ln_b_b, w_out_conv, w_pool,
           pool_scale):
    w = _prep_weights(norm_g, final_norm_g, w_ffn_gate, w_ffn_up, w_ffn_down, w_in_conv, conv_a_w, conv_b_w,
                      ln_b_g, ln_b_b, w_out_conv, w_pool, pool_scale)
    n_conv, n_pool = w_in_conv.shape[0], w_pool.shape[0]
    b = x_prompt.shape[0]
    z_a = jnp.zeros((n_conv, b, CONV_A_W - 1, D_MODEL), F32)
    z_b = jnp.zeros((n_conv, b, CONV_B_W - 1, D_MODEL), F32)
    z_p = jnp.zeros((n_pool, b, POOL_PREV, D_MODEL), F32)
    y_p, pa_a, pa_b, pa_p = _trunk(x_prompt, z_a, z_b, z_p, 0, w, m_conv=CONV_ROWS, m_pool=POOL_ROWS)
    db, dt, _ = x_sample.shape
    y_s, sa_a, sa_b, sa_p = _trunk(x_sample, state_conv_a, state_conv_b, state_pool, PAST_LEN, w,
                                   m_conv=db * dt, m_pool=db * dt)
    return (y_p, y_s, pa_a, sa_a, pa_b, sa_b, pa_p, sa_p)
```

```python
import functools

import jax
import jax.numpy as jnp
from jax import lax
from jax.experimental import pallas as pl
from jax.experimental.pallas import tpu as pltpu

D_MODEL = 1024
D_FF = 4 * D_MODEL
DEPTH = 4
CONV_A_W = 3
CONV_B_W = 31
POOL_WINDOWS = (2, 4, 8, 16)
POOL_G = D_MODEL // len(POOL_WINDOWS)
POOL_PREV = max(POOL_WINDOWS) - 1
PAST_LEN = 2048
RMS_EPS = 1e-6
LN_EPS = 1e-5

SUBLANES = 8
BF16_ROWS = 16
CAST_BLOCK_BYTES = 8 * 1024 * 1024
LANES = 128
MXU_N = 256
FFN_TM = 1024
FFN_CHUNK = 256
CONV_ROWS = 512
POOL_ROWS = 1024
CONV_OUT_BLOCK = 8
VMEM_LIMIT = 56 * 1024 * 1024

F32 = jnp.float32
BF16 = jnp.bfloat16


def _rms(x, g):
    ms = jnp.mean(x * x, axis=-1, keepdims=True)
    return x * lax.rsqrt(ms + RMS_EPS) * g


def _arr(arg):
    return arg[0] if isinstance(arg, tuple) else arg


def _resident(arg):
    arr, idx = arg if isinstance(arg, tuple) else (arg, ())
    rest = arr.shape[len(idx):]
    index = tuple(idx) + (0,) * len(rest)
    return pl.BlockSpec((None,) * len(idx) + rest, lambda *_: index, pipeline_mode=pl.Buffered(1))


def _cast_body(w_ref, o_ref):
    o_ref[...] = w_ref[...].astype(BF16)


def _to_bf16(w):
    shape = w.shape
    cols = shape[-1]
    w2 = w.reshape(-1, cols)
    rows = w2.shape[0]
    br = min(rows, 1 << ((CAST_BLOCK_BYTES // (cols * 4)).bit_length() - 1))
    assert rows % br == 0 and br % BF16_ROWS == 0
    blk = pl.BlockSpec((br, cols), lambda i: (i, 0))
    out = pl.pallas_call(
        _cast_body,
        grid=(rows // br,),
        in_specs=[blk],
        out_specs=blk,
        out_shape=jax.ShapeDtypeStruct(w2.shape, BF16),
        compiler_params=pltpu.CompilerParams(dimension_semantics=("arbitrary",),
                                             vmem_limit_bytes=VMEM_LIMIT),
        name="cast_bf16",
    )(w2)
    return out.reshape(shape)


def _ffn_body(x_ref, g_ref, wg_ref, wu_ref, wd_ref, fg_ref, o_ref, u_scr, *scratch, final, nb_in, nb_out):
    scratch = list(scratch)
    if nb_in:
        x_tm = scratch.pop(0)
        for t in range(x_ref.shape[1]):
            x_tm[t * nb_in:(t + 1) * nb_in, :] = x_ref[:, t, :]
    else:
        x_tm = x_ref
    acc = scratch.pop(0) if nb_out else o_ref

    u_scr[...] = _rms(x_tm[...], g_ref[...]).astype(BF16)
    for j in range(D_FF // FFN_CHUNK):
        ff = slice(j * FFN_CHUNK, (j + 1) * FFN_CHUNK)
        u = u_scr[...]
        gate = jnp.dot(u, wg_ref[:, ff], preferred_element_type=F32)
        up = jnp.dot(u, wu_ref[:, ff], preferred_element_type=F32)
        h = (gate * jax.nn.sigmoid(gate) * (0.5 * up)).astype(BF16)
        d = jnp.dot(h, wd_ref[ff, :], preferred_element_type=F32)
        if j == 0:
            acc[...] = x_tm[...] + d
        else:
            acc[...] += d
    if final:
        acc[...] = _rms(acc[...], fg_ref[...])
    if nb_out:
        for t in range(o_ref.shape[1]):
            o_ref[:, t, :] = acc[t * nb_out:(t + 1) * nb_out, :]


def _ffn(x, g, wg, wu, wd, fg, *, final, nb_in=None, nb_out=None):
    n = x.shape[0] * x.shape[1] if nb_in else x.shape[0]
    tm = min(FFN_TM, n)
    row = pl.BlockSpec((tm, D_MODEL), lambda i: (i, 0))

    def batch_major(nb):
        return pl.BlockSpec((nb, tm // nb, D_MODEL), lambda i: (0, i, 0))

    scratch = [pltpu.VMEM((tm, D_MODEL), BF16)]
    scratch += [pltpu.VMEM((tm, D_MODEL), F32)] * (bool(nb_in) + bool(nb_out))
    out_shape = (nb_out, n // nb_out, D_MODEL) if nb_out else (n, D_MODEL)
    return pl.pallas_call(
        functools.partial(_ffn_body, final=final, nb_in=nb_in, nb_out=nb_out),
        grid=(n // tm,),
        in_specs=[batch_major(nb_in) if nb_in else row] + [_resident(a) for a in (g, wg, wu, wd, fg)],
        out_specs=batch_major(nb_out) if nb_out else row,
        out_shape=jax.ShapeDtypeStruct(out_shape, F32),
        scratch_shapes=scratch,
        compiler_params=pltpu.CompilerParams(dimension_semantics=("arbitrary",),
                                             vmem_limit_bytes=VMEM_LIMIT),
        name="ffn_final" if final else "ffn",
    )(x, *[_arr(a) for a in (g, wg, wu, wd, fg)])


def _dwconv_block(w_ref, ext_ref, n_taps, nb, m, c0, width, emit):
    stride = nb // SUBLANES
    n_out = m // nb
    blk = min(CONV_OUT_BLOCK, n_out)
    for lane0 in range(c0, c0 + width, LANES):
        lanes = slice(lane0, lane0 + LANES)
        taps = [jnp.broadcast_to(w_ref[k:k + 1, lanes], (SUBLANES, LANES)) for k in range(n_taps)]
        for phase in range(stride):
            for i0 in range(0, n_out, blk):
                accs = [None] * blk
                for n in range(i0, i0 + blk + n_taps - 1):
                    r = (phase + stride * n) * SUBLANES
                    e = ext_ref[r:r + SUBLANES, lanes]
                    for ii in range(blk):
                        k = n - (i0 + ii)
                        if 0 <= k < n_taps:
                            term = taps[k] * e
                            accs[ii] = term if accs[ii] is None else accs[ii] + term
                for ii in range(blk):
                    emit((phase + stride * (i0 + ii)) * SUBLANES, lanes, accs[ii])


def _conv_body(x_ref, pa_ref, pb_ref, g_ref, win_ref, caw_ref, cbw_ref, lng_ref, lnb_ref, wout_ref,
               o_ref, na_ref, nb_ref, ext_a, ext_b, ya_scr, yb_scr, *, nb, m, carry):
    ha = (CONV_A_W - 1) * nb
    hb = (CONV_B_W - 1) * nb

    @pl.when(pl.program_id(0) == 0)
    def _():
        ext_a[0:ha, :] = pa_ref[...]
        ext_b[0:hb, :] = pb_ref[...]

    u = _rms(x_ref[...], g_ref[...]).astype(BF16)

    def proj(i, cols):
        return jnp.dot(u, win_ref[:, i * D_MODEL + cols.start:i * D_MODEL + cols.stop], preferred_element_type=F32)

    for c0 in range(0, D_MODEL, MXU_N):
        cols = slice(c0, c0 + MXU_N)
        ext_b[hb:hb + m, cols] = proj(3, cols) * jax.nn.sigmoid(proj(4, cols))

        def emit_b(r0, lanes, acc):
            yb_scr[r0:r0 + SUBLANES, lanes] = acc

        _dwconv_block(cbw_ref, ext_b, CONV_B_W, nb, m, c0, MXU_N, emit_b)

        ext_a[ha:ha + m, cols] = proj(2, cols) * proj(0, cols)
        ya_scr[:, cols] = proj(1, cols)

        def emit_a(r0, lanes, acc):
            ya_scr[r0:r0 + SUBLANES, lanes] = ya_scr[r0:r0 + SUBLANES, lanes] * acc

        _dwconv_block(caw_ref, ext_a, CONV_A_W, nb, m, c0, MXU_N, emit_a)

    yb = yb_scr[...]
    mu = jnp.mean(yb, axis=-1, keepdims=True)
    yc = yb - mu
    var = jnp.mean(yc * yc, axis=-1, keepdims=True)
    yb = yc * lax.rsqrt(var + LN_EPS) * lng_ref[...] + lnb_ref[...]
    yb = yb * jax.nn.sigmoid(yb)

    y = jnp.dot(ya_scr[...].astype(BF16), wout_ref[0], preferred_element_type=F32)
    y = y + jnp.dot(yb.astype(BF16), wout_ref[1], preferred_element_type=F32)
    o_ref[...] = x_ref[...] + y

    new_a = ext_a[m:m + ha, :]
    new_b = ext_b[m:m + hb, :]
    na_ref[...] = new_a
    nb_ref[...] = new_b
    if carry:
        ext_a[0:ha, :] = new_a
        ext_b[0:hb, :] = new_b


def _conv_mixer(x, prev_a, prev_b, g, w_in, caw, cbw, lng, lnb, w_out, *, nb, m):
    n = x.shape[0]
    ha = (CONV_A_W - 1) * nb
    hb = (CONV_B_W - 1) * nb
    n_tiles = n // m
    assert n_tiles == 1 or m >= hb
    body = functools.partial(_conv_body, nb=nb, m=m, carry=n_tiles > 1)
    row = pl.BlockSpec((m, D_MODEL), lambda i: (i, 0))
    return pl.pallas_call(
        body,
        grid=(n_tiles,),
        in_specs=[row] + [_resident(a) for a in (prev_a, prev_b, g, w_in, caw, cbw, lng, lnb, w_out)],
        out_specs=[row, pl.BlockSpec((ha, D_MODEL), lambda i: (0, 0)),
                   pl.BlockSpec((hb, D_MODEL), lambda i: (0, 0))],
        out_shape=[jax.ShapeDtypeStruct(x.shape, F32),
                   jax.ShapeDtypeStruct((ha, D_MODEL), F32),
                   jax.ShapeDtypeStruct((hb, D_MODEL), F32)],
        scratch_shapes=[pltpu.VMEM((ha + m, D_MODEL), F32), pltpu.VMEM((hb + m, D_MODEL), F32),
                        pltpu.VMEM((m, D_MODEL), F32), pltpu.VMEM((m, D_MODEL), F32)],
        compiler_params=pltpu.CompilerParams(dimension_semantics=("arbitrary",),
                                             vmem_limit_bytes=VMEM_LIMIT),
        name="conv_mixer",
    )(x, *[_arr(a) for a in (prev_a, prev_b, g, w_in, caw, cbw, lng, lnb, w_out)])


def _pool_body(x_ref, pp_ref, g_ref, wp_ref, sc_ref, o_ref, np_ref, ext, *, nb, m, start_pos, carry):
    hp = POOL_PREV * nb
    tt = m // nb
    t_idx = pl.program_id(0)

    @pl.when(t_idx == 0)
    def _():
        ext[0:hp, :] = pp_ref[...]

    x = x_ref[...]
    ext[hp:hp + m, :] = _rms(x, g_ref[...])

    log2_nb = nb.bit_length() - 1
    assert nb == 1 << log2_nb
    step = lax.shift_right_logical(lax.broadcasted_iota(jnp.int32, (m, 1), 0), log2_nb)
    pos = (start_pos + t_idx * tt + step).astype(F32)
    ys = []
    for gi, win in enumerate(POOL_WINDOWS):
        cols = slice(gi * POOL_G, (gi + 1) * POOL_G)
        s = None
        for k in range(win - 1, -1, -1):
            term = ext[hp - k * nb:hp - k * nb + m, cols]
            s = term if s is None else s + term
        cnt = jnp.minimum(jnp.float32(win), pos + 1.0)
        p = s / cnt - ext[hp:hp + m, cols]
        ys.append(jnp.dot(p.astype(BF16), wp_ref[gi], preferred_element_type=F32))
    y = jnp.concatenate(ys, axis=-1) * sc_ref[...]
    o_ref[...] = x + y

    new_p = ext[m:m + hp, :]
    np_ref[...] = new_p
    if carry:
        ext[0:hp, :] = new_p


def _pool_mixer(x, prev_p, g, wp, sc, *, nb, m, start_pos):
    n = x.shape[0]
    hp = POOL_PREV * nb
    n_tiles = n // m
    assert n_tiles == 1 or m >= hp
    body = functools.partial(_pool_body, nb=nb, m=m, start_pos=start_pos, carry=n_tiles > 1)
    row = pl.BlockSpec((m, D_MODEL), lambda i: (i, 0))
    return pl.pallas_call(
        body,
        grid=(n_tiles,),
        in_specs=[row] + [_resident(a) for a in (prev_p, g, wp, sc)],
        out_specs=[row, pl.BlockSpec((hp, D_MODEL), lambda i: (0, 0))],
        out_shape=[jax.ShapeDtypeStruct(x.shape, F32), jax.ShapeDtypeStruct((hp, D_MODEL), F32)],
        scratch_shapes=[pltpu.VMEM((hp + m, D_MODEL), F32)],
        compiler_params=pltpu.CompilerParams(dimension_semantics=("arbitrary",),
                                             vmem_limit_bytes=VMEM_LIMIT),
        name="pool_mixer",
    )(x, *[_arr(a) for a in (prev_p, g, wp, sc)])


def _time_major(a):
    a = jnp.swapaxes(a, -3, -2)
    return a.reshape(a.shape[:-3] + (a.shape[-3] * a.shape[-2], a.shape[-1]))


def _batch_major(a, nb):
    a = a.reshape(a.shape[:-2] + (a.shape[-2] // nb, nb, a.shape[-1]))
    return jnp.swapaxes(a, -3, -2)


def _trunk(x, st_a, st_b, st_p, start_pos, w, *, m_conv, m_pool):
    nb = x.shape[0]
    st_a, st_b, st_p = _time_major(st_a), _time_major(st_b), _time_major(st_p)
    new_a, new_b, new_p = [], [], []
    def pick(name, *idx):
        return (w[name], idx)

    for layer in range(DEPTH):
        i = layer // 2
        x = _ffn(x, pick("norm_g", layer, 0), pick("wg", layer, 0), pick("wu", layer, 0), pick("wd", layer, 0),
                 w["final_g"], final=False, nb_in=nb if layer == 0 else None)
        if layer % 2 == 0:
            x, sa, sb = _conv_mixer(x, (st_a, (i,)), (st_b, (i,)), pick("norm_g", layer, 1), pick("w_in", i),
                                    pick("caw", i), pick("cbw", i), pick("lng", i), pick("lnb", i),
                                    pick("w_out", i), nb=nb, m=m_conv)
            new_a.append(sa)
            new_b.append(sb)
        else:
            x, sp = _pool_mixer(x, (st_p, (i,)), pick("norm_g", layer, 1), pick("wp", i), pick("psc", i),
                                nb=nb, m=m_pool, start_pos=start_pos)
            new_p.append(sp)
        last = layer == DEPTH - 1
        x = _ffn(x, pick("norm_g", layer, 2), pick("wg", layer, 1), pick("wu", layer, 1), pick("wd", layer, 1),
                 w["final_g"], final=last, nb_out=nb if last else None)
    return (x, _batch_major(jnp.stack(new_a), nb), _batch_major(jnp.stack(new_b), nb),
            _batch_major(jnp.stack(new_p), nb))


def _prep_weights(norm_g, final_norm_g, w_ffn_gate, w_ffn_up, w_ffn_down, w_in_conv, conv_a_w, conv_b_w,
                  ln_b_g, ln_b_b, w_out_conv, w_pool, pool_scale):
    n_conv = w_in_conv.shape[0]
    return {
        "norm_g": norm_g.reshape(DEPTH, 3, 1, D_MODEL),
        "final_g": final_norm_g.reshape(1, D_MODEL),
        "wg": _to_bf16(w_ffn_gate),
        "wu": _to_bf16(w_ffn_up),
        "wd": _to_bf16(w_ffn_down),
        "w_in": _to_bf16(w_in_conv),
        "caw": conv_a_w,
        "cbw": conv_b_w,
        "lng": ln_b_g.reshape(n_conv, 1, D_MODEL),
        "lnb": ln_b_b.reshape(n_conv, 1, D_MODEL),
        "w_out": _to_bf16(w_out_conv).reshape(n_conv, 2, D_MODEL, D_MODEL),
        "wp": _to_bf16(w_pool),
        "psc": pool_scale.reshape(-1, 1, D_MODEL),
    }


def kernel(x_prompt, x_sample, state_conv_a, state_conv_b, state_pool, norm_g, final_norm_g, w_ffn_gate,
           w_ffn_up, w_ffn_down, w_in_conv, conv_a_w, conv_b_w, ln_b_g, ln_b_b, w_out_conv, w_pool,
           pool_scale):
    w = _prep_weights(norm_g, final_norm_g, w_ffn_gate, w_ffn_up, w_ffn_down, w_in_conv, conv_a_w, conv_b_w,
                      ln_b_g, ln_b_b, w_out_conv, w_pool, pool_scale)
    n_conv, n_pool = w_in_conv.shape[0], w_pool.shape[0]
    b = x_prompt.shape[0]
    z_a = jnp.zeros((n_conv, b, CONV_A_W - 1, D_MODEL), F32)
    z_b = jnp.zeros((n_conv, b, CONV_B_W - 1, D_MODEL), F32)
    z_p = jnp.zeros((n_pool, b, POOL_PREV, D_MODEL), F32)
    y_p, pa_a, pa_b, pa_p = _trunk(x_prompt, z_a, z_b, z_p, 0, w, m_conv=CONV_ROWS, m_pool=POOL_ROWS)
    db, dt, _ = x_sample.shape
    y_s, sa_a, sa_b, sa_p = _trunk(x_sample, state_conv_a, state_conv_b, state_pool, PAST_LEN, w,
                                   m_conv=db * dt, m_pool=db * dt)
    return (y_p, y_s, pa_a, sa_a, pa_b, sa_b, pa_p, sa_p)
```

```python
import functools

import jax
import jax.numpy as jnp
from jax import lax
from jax.experimental import pallas as pl
from jax.experimental.pallas import tpu as pltpu

D_MODEL = 1024
D_FF = 4 * D_MODEL
DEPTH = 4
CONV_A_W = 3
CONV_B_W = 31
POOL_WINDOWS = (2, 4, 8, 16)
POOL_G = D_MODEL // len(POOL_WINDOWS)
POOL_PREV = max(POOL_WINDOWS) - 1
PAST_LEN = 2048
RMS_EPS = 1e-6
LN_EPS = 1e-5

SUBLANES = 8
BF16_ROWS = 16
CAST_BLOCK_BYTES = 8 * 1024 * 1024
LANES = 128
MXU_N = 256
FFN_TM = 1024
FFN_CHUNK = 256
CONV_ROWS = 1024
POOL_ROWS = 1024
CONV_OUT_BLOCK = 8
VMEM_LIMIT = 56 * 1024 * 1024

F32 = jnp.float32
BF16 = jnp.bfloat16


def _rms(x, g):
    ms = jnp.mean(x * x, axis=-1, keepdims=True)
    return x * lax.rsqrt(ms + RMS_EPS) * g


def _arr(arg):
    return arg[0] if isinstance(arg, tuple) else arg


def _resident(arg):
    arr, idx = arg if isinstance(arg, tuple) else (arg, ())
    rest = arr.shape[len(idx):]
    index = tuple(idx) + (0,) * len(rest)
    return pl.BlockSpec((None,) * len(idx) + rest, lambda *_: index, pipeline_mode=pl.Buffered(1))


def _cast_body(w_ref, o_ref):
    o_ref[...] = w_ref[...].astype(BF16)


def _to_bf16(w):
    shape = w.shape
    cols = shape[-1]
    w2 = w.reshape(-1, cols)
    rows = w2.shape[0]
    br = min(rows, 1 << ((CAST_BLOCK_BYTES // (cols * 4)).bit_length() - 1))
    assert rows % br == 0 and br % BF16_ROWS == 0
    blk = pl.BlockSpec((br, cols), lambda i: (i, 0))
    out = pl.pallas_call(
        _cast_body,
        grid=(rows // br,),
        in_specs=[blk],
        out_specs=blk,
        out_shape=jax.ShapeDtypeStruct(w2.shape, BF16),
        compiler_params=pltpu.CompilerParams(dimension_semantics=("arbitrary",),
                                             vmem_limit_bytes=VMEM_LIMIT),
        name="cast_bf16",
    )(w2)
    return out.reshape(shape)


def _ffn_body(x_ref, g_ref, wg_ref, wu_ref, wd_ref, fg_ref, o_ref, u_scr, *scratch, final, nb_in, nb_out):
    scratch = list(scratch)
    if nb_in:
        x_tm = scratch.pop(0)
        for t in range(x_ref.shape[1]):
            x_tm[t * nb_in:(t + 1) * nb_in, :] = x_ref[:, t, :]
    else:
        x_tm = x_ref
    acc = scratch.pop(0) if nb_out else o_ref

    u_scr[...] = _rms(x_tm[...], g_ref[...]).astype(BF16)
    for j in range(D_FF // FFN_CHUNK):
        ff = slice(j * FFN_CHUNK, (j + 1) * FFN_CHUNK)
        u = u_scr[...]
        gate = jnp.dot(u, wg_ref[:, ff], preferred_element_type=F32)
        up = jnp.dot(u, wu_ref[:, ff], preferred_element_type=F32)
        h = (gate * jax.nn.sigmoid(gate) * (0.5 * up)).astype(BF16)
        d = jnp.dot(h, wd_ref[ff, :], preferred_element_type=F32)
        if j == 0:
            acc[...] = x_tm[...] + d
        else:
            acc[...] += d
    if final:
        acc[...] = _rms(acc[...], fg_ref[...])
    if nb_out:
        for t in range(o_ref.shape[1]):
            o_ref[:, t, :] = acc[t * nb_out:(t + 1) * nb_out, :]


def _ffn(x, g, wg, wu, wd, fg, *, final, nb_in=None, nb_out=None):
    n = x.shape[0] * x.shape[1] if nb_in else x.shape[0]
    tm = min(FFN_TM, n)
    row = pl.BlockSpec((tm, D_MODEL), lambda i: (i, 0))

    def batch_major(nb):
        return pl.BlockSpec((nb, tm // nb, D_MODEL), lambda i: (0, i, 0))

    scratch = [pltpu.VMEM((tm, D_MODEL), BF16)]
    scratch += [pltpu.VMEM((tm, D_MODEL), F32)] * (bool(nb_in) + bool(nb_out))
    out_shape = (nb_out, n // nb_out, D_MODEL) if nb_out else (n, D_MODEL)
    return pl.pallas_call(
        functools.partial(_ffn_body, final=final, nb_in=nb_in, nb_out=nb_out),
        grid=(n // tm,),
        in_specs=[batch_major(nb_in) if nb_in else row] + [_resident(a) for a in (g, wg, wu, wd, fg)],
        out_specs=batch_major(nb_out) if nb_out else row,
        out_shape=jax.ShapeDtypeStruct(out_shape, F32),
        scratch_shapes=scratch,
        compiler_params=pltpu.CompilerParams(dimension_semantics=("arbitrary",),
                                             vmem_limit_bytes=VMEM_LIMIT),
        name="ffn_final" if final else "ffn",
    )(x, *[_arr(a) for a in (g, wg, wu, wd, fg)])


def _dwconv_block(w_ref, ext_ref, n_taps, nb, m, c0, width, emit):
    stride = nb // SUBLANES
    n_out = m // nb
    blk = min(CONV_OUT_BLOCK, n_out)
    for lane0 in range(c0, c0 + width, LANES):
        lanes = slice(lane0, lane0 + LANES)
        taps = [jnp.broadcast_to(w_ref[k:k + 1, lanes], (SUBLANES, LANES)) for k in range(n_taps)]
        for phase in range(stride):
            for i0 in range(0, n_out, blk):
                accs = [None] * blk
                for n in range(i0, i0 + blk + n_taps - 1):
                    r = (phase + stride * n) * SUBLANES
                    e = ext_ref[r:r + SUBLANES, lanes]
                    for ii in range(blk):
                        k = n - (i0 + ii)
                        if 0 <= k < n_taps:
                            term = taps[k] * e
                            accs[ii] = term if accs[ii] is None else accs[ii] + term
                for ii in range(blk):
                    emit((phase + stride * (i0 + ii)) * SUBLANES, lanes, accs[ii])


def _conv_body(x_ref, pa_ref, pb_ref, g_ref, win_ref, caw_ref, cbw_ref, lng_ref, lnb_ref, wout_ref,
               o_ref, na_ref, nb_ref, ext_a, ext_b, ya_scr, yb_scr, *, nb, m, carry):
    ha = (CONV_A_W - 1) * nb
    hb = (CONV_B_W - 1) * nb

    @pl.when(pl.program_id(0) == 0)
    def _():
        ext_a[0:ha, :] = pa_ref[...]
        ext_b[0:hb, :] = pb_ref[...]

    u = _rms(x_ref[...], g_ref[...]).astype(BF16)

    def proj(i, cols):
        return jnp.dot(u, win_ref[:, i * D_MODEL + cols.start:i * D_MODEL + cols.stop], preferred_element_type=F32)

    for c0 in range(0, D_MODEL, MXU_N):
        cols = slice(c0, c0 + MXU_N)
        ext_b[hb:hb + m, cols] = proj(3, cols) * jax.nn.sigmoid(proj(4, cols))

        def emit_b(r0, lanes, acc):
            yb_scr[r0:r0 + SUBLANES, lanes] = acc

        _dwconv_block(cbw_ref, ext_b, CONV_B_W, nb, m, c0, MXU_N, emit_b)

        ext_a[ha:ha + m, cols] = proj(2, cols) * proj(0, cols)
        ya_scr[:, cols] = proj(1, cols)

        def emit_a(r0, lanes, acc):
            ya_scr[r0:r0 + SUBLANES, lanes] = ya_scr[r0:r0 + SUBLANES, lanes] * acc

        _dwconv_block(caw_ref, ext_a, CONV_A_W, nb, m, c0, MXU_N, emit_a)

    yb = yb_scr[...]
    mu = jnp.mean(yb, axis=-1, keepdims=True)
    yc = yb - mu
    var = jnp.mean(yc * yc, axis=-1, keepdims=True)
    yb = yc * lax.rsqrt(var + LN_EPS) * lng_ref[...] + lnb_ref[...]
    yb = yb * jax.nn.sigmoid(yb)

    y = jnp.dot(ya_scr[...].astype(BF16), wout_ref[0], preferred_element_type=F32)
    y = y + jnp.dot(yb.astype(BF16), wout_ref[1], preferred_element_type=F32)
    o_ref[...] = x_ref[...] + y

    new_a = ext_a[m:m + ha, :]
    new_b = ext_b[m:m + hb, :]
    na_ref[...] = new_a
    nb_ref[...] = new_b
    if carry:
        ext_a[0:ha, :] = new_a
        ext_b[0:hb, :] = new_b


def _conv_mixer(x, prev_a, prev_b, g, w_in, caw, cbw, lng, lnb, w_out, *, nb, m):
    n = x.shape[0]
    ha = (CONV_A_W - 1) * nb
    hb = (CONV_B_W - 1) * nb
    n_tiles = n // m
    assert n_tiles == 1 or m >= hb
    body = functools.partial(_conv_body, nb=nb, m=m, carry=n_tiles > 1)
    row = pl.BlockSpec((m, D_MODEL), lambda i: (i, 0))
    return pl.pallas_call(
        body,
        grid=(n_tiles,),
        in_specs=[row] + [_resident(a) for a in (prev_a, prev_b, g, w_in, caw, cbw, lng, lnb, w_out)],
        out_specs=[row, pl.BlockSpec((ha, D_MODEL), lambda i: (0, 0)),
                   pl.BlockSpec((hb, D_MODEL), lambda i: (0, 0))],
        out_shape=[jax.ShapeDtypeStruct(x.shape, F32),
                   jax.ShapeDtypeStruct((ha, D_MODEL), F32),
                   jax.ShapeDtypeStruct((hb, D_MODEL), F32)],
        scratch_shapes=[pltpu.VMEM((ha + m, D_MODEL), F32), pltpu.VMEM((hb + m, D_MODEL), F32),
                        pltpu.VMEM((m, D_MODEL), F32), pltpu.VMEM((m, D_MODEL), F32)],
        compiler_params=pltpu.CompilerParams(dimension_semantics=("arbitrary",),
                                             vmem_limit_bytes=VMEM_LIMIT),
        name="conv_mixer",
    )(x, *[_arr(a) for a in (prev_a, prev_b, g, w_in, caw, cbw, lng, lnb, w_out)])


def _pool_body(x_ref, pp_ref, g_ref, wp_ref, sc_ref, o_ref, np_ref, ext, *, nb, m, start_pos, carry):
    hp = POOL_PREV * nb
    tt = m // nb
    t_idx = pl.program_id(0)

    @pl.when(t_idx == 0)
    def _():
        ext[0:hp, :] = pp_ref[...]

    x = x_ref[...]
    ext[hp:hp + m, :] = _rms(x, g_ref[...])

    log2_nb = nb.bit_length() - 1
    assert nb == 1 << log2_nb
    step = lax.shift_right_logical(lax.broadcasted_iota(jnp.int32, (m, 1), 0), log2_nb)
    pos = (start_pos + t_idx * tt + step).astype(F32)
    ys = []
    for gi, win in enumerate(POOL_WINDOWS):
        cols = slice(gi * POOL_G, (gi + 1) * POOL_G)
        s = None
        for k in range(win - 1, -1, -1):
            term = ext[hp - k * nb:hp - k * nb + m, cols]
            s = term if s is None else s + term
        cnt = jnp.minimum(jnp.float32(win), pos + 1.0)
        p = s / cnt - ext[hp:hp + m, cols]
        ys.append(jnp.dot(p.astype(BF16), wp_ref[gi], preferred_element_type=F32))
    y = jnp.concatenate(ys, axis=-1) * sc_ref[...]
    o_ref[...] = x + y

    new_p = ext[m:m + hp, :]
    np_ref[...] = new_p
    if carry:
        ext[0:hp, :] = new_p


def _pool_mixer(x, prev_p, g, wp, sc, *, nb, m, start_pos):
    n = x.shape[0]
    hp = POOL_PREV * nb
    n_tiles = n // m
    assert n_tiles == 1 or m >= hp
    body = functools.partial(_pool_body, nb=nb, m=m, start_pos=start_pos, carry=n_tiles > 1)
    row = pl.BlockSpec((m, D_MODEL), lambda i: (i, 0))
    return pl.pallas_call(
        body,
        grid=(n_tiles,),
        in_specs=[row] + [_resident(a) for a in (prev_p, g, wp, sc)],
        out_specs=[row, pl.BlockSpec((hp, D_MODEL), lambda i: (0, 0))],
        out_shape=[jax.ShapeDtypeStruct(x.shape, F32), jax.ShapeDtypeStruct((hp, D_MODEL), F32)],
        scratch_shapes=[pltpu.VMEM((hp + m, D_MODEL), F32)],
        compiler_params=pltpu.CompilerParams(dimension_semantics=("arbitrary",),
                                             vmem_limit_bytes=VMEM_LIMIT),
        name="pool_mixer",
    )(x, *[_arr(a) for a in (prev_p, g, wp, sc)])


def _time_major(a):
    a = jnp.swapaxes(a, -3, -2)
    return a.reshape(a.shape[:-3] + (a.shape[-3] * a.shape[-2], a.shape[-1]))


def _batch_major(a, nb):
    a = a.reshape(a.shape[:-2] + (a.shape[-2] // nb, nb, a.shape[-1]))
    return jnp.swapaxes(a, -3, -2)


def _trunk(x, st_a, st_b, st_p, start_pos, w, *, m_conv, m_pool):
    nb = x.shape[0]
    st_a, st_b, st_p = _time_major(st_a), _time_major(st_b), _time_major(st_p)
    new_a, new_b, new_p = [], [], []
    def pick(name, *idx):
        return (w[name], idx)

    for layer in range(DEPTH):
        i = layer // 2
        x = _ffn(x, pick("norm_g", layer, 0), pick("wg", layer, 0), pick("wu", layer, 0), pick("wd", layer, 0),
                 w["final_g"], final=False, nb_in=nb if layer == 0 else None)
        if layer % 2 == 0:
            x, sa, sb = _conv_mixer(x, (st_a, (i,)), (st_b, (i,)), pick("norm_g", layer, 1), pick("w_in", i),
                                    pick("caw", i), pick("cbw", i), pick("lng", i), pick("lnb", i),
                                    pick("w_out", i), nb=nb, m=m_conv)
            new_a.append(sa)
            new_b.append(sb)
        else:
            x, sp = _pool_mixer(x, (st_p, (i,)), pick("norm_g", layer, 1), pick("wp", i), pick("psc", i),
                                nb=nb, m=m_pool, start_pos=start_pos)
            new_p.append(sp)
        last = layer == DEPTH - 1
        x = _ffn(x, pick("norm_g", layer, 2), pick("wg", layer, 1), pick("wu", layer, 1), pick("wd", layer, 1),
                 w["final_g"], final=last, nb_out=nb if last else None)
    return (x, _batch_major(jnp.stack(new_a), nb), _batch_major(jnp.stack(new_b), nb),
            _batch_major(jnp.stack(new_p), nb))


def _prep_weights(norm_g, final_norm_g, w_ffn_gate, w_ffn_up, w_ffn_down, w_in_conv, conv_a_w, conv_b_w,
                  ln_b_g, ln_b_b, w_out_conv, w_pool, pool_scale):
    n_conv = w_in_conv.shape[0]
    return {
        "norm_g": norm_g.reshape(DEPTH, 3, 1, D_MODEL),
        "final_g": final_norm_g.reshape(1, D_MODEL),
        "wg": _to_bf16(w_ffn_gate),
        "wu": _to_bf16(w_ffn_up),
        "wd": _to_bf16(w_ffn_down),
        "w_in": _to_bf16(w_in_conv),
        "caw": conv_a_w,
        "cbw": conv_b_w,
        "lng": ln_b_g.reshape(n_conv, 1, D_MODEL),
        "lnb": ln_b_b.reshape(n_conv, 1, D_MODEL),
        "w_out": _to_bf16(w_out_conv).reshape(n_conv, 2, D_MODEL, D_MODEL),
        "wp": _to_bf16(w_pool),
        "psc": pool_scale.reshape(-1, 1, D_MODEL),
    }


def kernel(x_prompt, x_sample, state_conv_a, state_conv_b, state_pool, norm_g, final_norm_g, w_ffn_gate,
           w_ffn_up, w_ffn_down, w_in_conv, conv_a_w, conv_b_w, ln_b_g, ln_b_b, w_out_conv, w_pool,
           pool_scale):
    w = _prep_weights(norm_g, final_norm_g, w_ffn_gate, w_ffn_up, w_ffn_down, w_in_conv, conv_a_w, conv_b_w,
                      ln_b_g, ln_b_b, w_out_conv, w_pool, pool_scale)
    n_conv, n_pool = w_in_conv.shape[0], w_pool.shape[0]
    b = x_prompt.shape[0]
    z_a = jnp.zeros((n_conv, b, CONV_A_W - 1, D_MODEL), F32)
    z_b = jnp.zeros((n_conv, b, CONV_B_W - 1, D_MODEL), F32)
    z_p = jnp.zeros((n_pool, b, POOL_PREV, D_MODEL), F32)
    y_p, pa_a, pa_b, pa_p = _trunk(x_prompt, z_a, z_b, z_p, 0, w, m_conv=CONV_ROWS, m_pool=POOL_ROWS)
    db, dt, _ = x_sample.shape
    y_s, sa_a, sa_b, sa_p = _trunk(x_sample, state_conv_a, state_conv_b, state_pool, PAST_LEN, w,
                                   m_conv=db * dt, m_pool=db * dt)
    return (y_p, y_s, pa_a, sa_a, pa_b, sa_b, pa_p, sa_p)
```
